```python
import math
import jax
import jax.numpy as jnp
from jax import lax
import numpy as np

D_MODEL = 2048
BATCH = 4
SEQ = 2048
DEPTH = 2

N_MIXERS = 2
N_ATTN_LAYERS = (DEPTH + 1) // 2
N_GDN_LAYERS = DEPTH // 2
D_FF = 256 * ((8 * D_MODEL // 3 + 255) // 256)
RMS_EPS = 1e-6
DA_HEAD_DIM = 128
DA_HEADS = D_MODEL // (2 * DA_HEAD_DIM)
ROPE_THETA = 10000.0
Q_BLOCK = 128
GDN_HEAD_DIM = 128
GDN_K_HEADS = D_MODEL // GDN_HEAD_DIM
GDN_V_HEADS = 2 * GDN_K_HEADS
GDN_KEY_DIM = GDN_K_HEADS * GDN_HEAD_DIM
GDN_VAL_DIM = GDN_V_HEADS * GDN_HEAD_DIM
GDN_CONV_DIM = 2 * GDN_KEY_DIM + GDN_VAL_DIM
GDN_CONV_WIDTH = 4
GDN_IN_DIM = GDN_CONV_DIM + GDN_VAL_DIM + 2 * GDN_V_HEADS
CHUNK = 64

kernel_name = 'hybrid_diffattn_gated_deltanet_macaron'


def rms_norm(x, w):
    xf = x.astype(jnp.float32)
    y = xf * lax.rsqrt(jnp.mean(xf * xf, axis=-1, keepdims=True) + RMS_EPS)
    return (y * w.astype(jnp.float32)).astype(x.dtype)


def swiglu(x, w_gate, w_up, w_down):
    return (jax.nn.silu(x @ w_gate) * (x @ w_up)) @ w_down


def rope_tables(seq, dim):
    inv = 1.0 / (ROPE_THETA ** (jnp.arange(0, dim, 2, dtype=jnp.float32) / dim))
    ang = jnp.arange(seq, dtype=jnp.float32)[:, None] * inv[None, :]
    ang = jnp.concatenate([ang, ang], axis=-1)
    return jnp.cos(ang), jnp.sin(ang)


def apply_rope(x, cos, sin):
    xf = x.astype(jnp.float32)
    half = xf.shape[-1] // 2
    rot = jnp.concatenate([-xf[..., half:], xf[..., :half]], axis=-1)
    return (xf * cos[None, :, None, :] + rot * sin[None, :, None, :]).astype(x.dtype)


def diff_attention(h, w_qkv, lq1, lk1, lq2, lk2, subln, w_o, lambda_init):
    B, S, _ = h.shape
    q, k, v = jnp.split(h @ w_qkv, 3, axis=-1)
    q = q.reshape(B, S, 2 * DA_HEADS, DA_HEAD_DIM)
    k = k.reshape(B, S, 2 * DA_HEADS, DA_HEAD_DIM)
    v = v.reshape(B, S, DA_HEADS, 2 * DA_HEAD_DIM)
    cos, sin = rope_tables(S, DA_HEAD_DIM)
    q = apply_rope(q, cos, sin)
    k = apply_rope(k, cos, sin)
    lam = (jnp.exp(jnp.sum(lq1.astype(jnp.float32) * lk1.astype(jnp.float32)))
           - jnp.exp(jnp.sum(lq2.astype(jnp.float32) * lk2.astype(jnp.float32))) + lambda_init)
    scale = DA_HEAD_DIM ** -0.5
    k_t = jnp.transpose(k, (0, 2, 1, 3))
    v_t = jnp.transpose(v, (0, 2, 1, 3))
    n_blocks = S // Q_BLOCK
    q_blocks = jnp.transpose(q.reshape(B, n_blocks, Q_BLOCK, 2 * DA_HEADS, DA_HEAD_DIM), (1, 0, 3, 2, 4))
    key_pos = jnp.arange(S)
    neg = jnp.finfo(jnp.float32).min

    def one_block(args):
        qb, start = args
        s = jnp.einsum('bhqd,bhkd->bhqk', qb, k_t).astype(jnp.float32) * scale
        q_pos = start + jnp.arange(Q_BLOCK)
        s = jnp.where(key_pos[None, :] <= q_pos[:, None], s, neg)
        p = jax.nn.softmax(s, axis=-1).reshape(B, DA_HEADS, 2, Q_BLOCK, S)
        a = p[:, :, 0] - lam * p[:, :, 1]
        return jnp.einsum('bhqk,bhkd->bhqd', a.astype(v_t.dtype), v_t)

    o = lax.map(one_block, (q_blocks, jnp.arange(n_blocks) * Q_BLOCK))
    o = jnp.transpose(o, (1, 0, 3, 2, 4)).reshape(B, S, DA_HEADS, 2 * DA_HEAD_DIM)
    o = rms_norm(o, subln) * (1.0 - lambda_init)
    return o.reshape(B, S, D_MODEL) @ w_o


def causal_depthwise_conv(x, w):
    return lax.conv_general_dilated(
        x, w[:, None, :].astype(x.dtype), window_strides=(1,),
        padding=[(GDN_CONV_WIDTH - 1, 0)], dimension_numbers=('NWC', 'WIO', 'NWC'),
        feature_group_count=x.shape[-1])


def l2_normalize(x):
    return x * lax.rsqrt(jnp.sum(x * x, axis=-1, keepdims=True) + 1e-6)


def chunk_gated_delta_rule(q, k, v, g, beta):
    B, T, H, dk = q.shape
    dv = v.shape[-1]
    n = T // CHUNK

    def chunks(t):
        t = jnp.moveaxis(t, 2, 1)
        return t.reshape((B, H, n, CHUNK) + t.shape[3:])

    q = chunks(q) * (dk ** -0.5)
    k, v, g, beta = chunks(k), chunks(v), chunks(g), chunks(beta)
    g = jnp.cumsum(g, axis=-1)
    tril = jnp.tril(jnp.ones((CHUNK, CHUNK), dtype=bool))
    strict = jnp.tril(jnp.ones((CHUNK, CHUNK), dtype=bool), k=-1)
    gdiff = g[..., :, None] - g[..., None, :]
    decay = jnp.where(tril, jnp.exp(jnp.where(tril, gdiff, 0.0)), 0.0)
    k_beta = k * beta[..., None]
    v_beta = v * beta[..., None]
    a_mat = jnp.where(strict, jnp.einsum('bhncd,bhnsd->bhncs', k_beta, k) * decay, 0.0)
    eye = jnp.eye(CHUNK, dtype=jnp.float32)
    t_mat = lax.linalg.triangular_solve(a_mat + eye, jnp.broadcast_to(eye, a_mat.shape),
                                        left_side=True, lower=True, unit_diagonal=True)
    u = jnp.einsum('bhncs,bhnsd->bhncd', t_mat, v_beta)
    w = jnp.einsum('bhncs,bhnsd->bhncd', t_mat, k_beta * jnp.exp(g)[..., None])
    qk = jnp.where(tril, jnp.einsum('bhncd,bhnsd->bhncs', q, k) * decay, 0.0)
    q_g = q * jnp.exp(g)[..., None]
    k_g = k * jnp.exp(g[..., -1:] - g)[..., None]
    g_last = jnp.exp(g[..., -1])
    xs = tuple(jnp.moveaxis(t, 2, 0) for t in (q_g, k_g, u, w, qk, g_last))

    def step(state, inp):
        q_c, k_c, u_c, w_c, qk_c, gl = inp
        v_new = u_c - jnp.einsum('bhcd,bhde->bhce', w_c, state)
        o = jnp.einsum('bhcd,bhde->bhce', q_c, state) + jnp.einsum('bhcs,bhse->bhce', qk_c, v_new)
        state = state * gl[..., None, None] + jnp.einsum('bhcd,bhce->bhde', k_c, v_new)
        return state, o

    _, o = lax.scan(step, jnp.zeros((B, H, dk, dv), jnp.float32), xs)
    return jnp.transpose(o, (1, 0, 3, 2, 4)).reshape(B, T, H, dv)


def gated_deltanet(h, w_in, conv_w, a_log, dt_bias, norm_w, w_o):
    B, S, _ = h.shape
    proj = h @ w_in
    qkv, z, b, a = jnp.split(proj, [GDN_CONV_DIM, GDN_CONV_DIM + GDN_VAL_DIM,
                                    GDN_CONV_DIM + GDN_VAL_DIM + GDN_V_HEADS], axis=-1)
    qkv = jax.nn.silu(causal_depthwise_conv(qkv, conv_w)).astype(jnp.float32)
    q, k, v = jnp.split(qkv, [GDN_KEY_DIM, 2 * GDN_KEY_DIM], axis=-1)
    q = l2_normalize(q.reshape(B, S, GDN_K_HEADS, GDN_HEAD_DIM))
    k = l2_normalize(k.reshape(B, S, GDN_K_HEADS, GDN_HEAD_DIM))
    v = v.reshape(B, S, GDN_V_HEADS, GDN_HEAD_DIM)
    rep = GDN_V_HEADS // GDN_K_HEADS
    q = jnp.repeat(q, rep, axis=2)
    k = jnp.repeat(k, rep, axis=2)
    beta = jax.nn.sigmoid(b.astype(jnp.float32))
    g = -jnp.exp(a_log.astype(jnp.float32)) * jax.nn.softplus(
        a.astype(jnp.float32) + dt_bias.astype(jnp.float32))
    o = chunk_gated_delta_rule(q, k, v, g, beta)
    zf = z.reshape(B, S, GDN_V_HEADS, GDN_HEAD_DIM).astype(jnp.float32)
    o = (o * lax.rsqrt(jnp.mean(o * o, axis=-1, keepdims=True) + RMS_EPS)
         * norm_w.astype(jnp.float32) * jax.nn.silu(zf))
    return o.astype(h.dtype).reshape(B, S, GDN_VAL_DIM) @ w_o


def setup_inputs(seed: int = 0) -> dict:
    key = jax.random.key(seed)
    ks = iter(jax.random.split(key, 32))

    def nrm(shape, scale):
        return scale * jax.random.normal(next(ks), shape, jnp.float32)

    def gain(shape):
        return 1.0 + 0.05 * jax.random.normal(next(ks), shape, jnp.float32)

    D, F = D_MODEL, D_FF
    return {
        'x': nrm((BATCH, SEQ, D), 1.0),
        'ffn1_norm': gain((DEPTH, D)),
        'ffn1_w_gate': nrm((DEPTH, D, F), D ** -0.5),
        'ffn1_w_up': nrm((DEPTH, D, F), D ** -0.5),
        'ffn1_w_down': nrm((DEPTH, F, D), F ** -0.5),
        'mix_norm': gain((DEPTH, D)),
        'ffn2_norm': gain((DEPTH, D)),
        'ffn2_w_gate': nrm((DEPTH, D, F), D ** -0.5),
        'ffn2_w_up': nrm((DEPTH, D, F), D ** -0.5),
        'ffn2_w_down': nrm((DEPTH, F, D), F ** -0.5),
        'da_w_qkv': nrm((N_ATTN_LAYERS, D, 3 * D), D ** -0.5),
        'da_lambda_q1': nrm((N_ATTN_LAYERS, DA_HEAD_DIM), 0.1),
        'da_lambda_k1': nrm((N_ATTN_LAYERS, DA_HEAD_DIM), 0.1),
        'da_lambda_q2': nrm((N_ATTN_LAYERS, DA_HEAD_DIM), 0.1),
        'da_lambda_k2': nrm((N_ATTN_LAYERS, DA_HEAD_DIM), 0.1),
        'da_subln': gain((N_ATTN_LAYERS, 2 * DA_HEAD_DIM)),
        'da_w_o': nrm((N_ATTN_LAYERS, D, D), D ** -0.5),
        'gdn_w_in': nrm((N_GDN_LAYERS, D, GDN_IN_DIM), D ** -0.5),
        'gdn_conv_w': nrm((N_GDN_LAYERS, GDN_CONV_WIDTH, GDN_CONV_DIM), GDN_CONV_WIDTH ** -0.5),
        'gdn_a_log': jnp.log(jax.random.uniform(next(ks), (N_GDN_LAYERS, GDN_V_HEADS), jnp.float32, 1.0, 16.0)),
        'gdn_dt_bias': nrm((N_GDN_LAYERS, GDN_V_HEADS), 0.5),
        'gdn_norm': gain((N_GDN_LAYERS, GDN_HEAD_DIM)),
        'gdn_w_o': nrm((N_GDN_LAYERS, GDN_VAL_DIM, D), GDN_VAL_DIM ** -0.5),
        'final_norm': gain((D,)),
    }


def reference(x, ffn1_norm, ffn1_w_gate, ffn1_w_up, ffn1_w_down, mix_norm,
              ffn2_norm, ffn2_w_gate, ffn2_w_up, ffn2_w_down,
              da_w_qkv, da_lambda_q1, da_lambda_k1, da_lambda_q2, da_lambda_k2, da_subln, da_w_o,
              gdn_w_in, gdn_conv_w, gdn_a_log, gdn_dt_bias, gdn_norm, gdn_w_o, final_norm):
    h = x
    for i in range(DEPTH):
        h = h + 0.5 * swiglu(rms_norm(h, ffn1_norm[i]), ffn1_w_gate[i], ffn1_w_up[i], ffn1_w_down[i])
        hn = rms_norm(h, mix_norm[i])
        j = i // N_MIXERS
        if i % N_MIXERS == 0:
            lambda_init = 0.8 - 0.6 * math.exp(-0.3 * i)
            mix = diff_attention(hn, da_w_qkv[j], da_lambda_q1[j], da_lambda_k1[j],
                                 da_lambda_q2[j], da_lambda_k2[j], da_subln[j], da_w_o[j], lambda_init)
        else:
            mix = gated_deltanet(hn, gdn_w_in[j], gdn_conv_w[j], gdn_a_log[j], gdn_dt_bias[j],
                                 gdn_norm[j], gdn_w_o[j])
        h = h + mix
        h = h + 0.5 * swiglu(rms_norm(h, ffn2_norm[i]), ffn2_w_gate[i], ffn2_w_up[i], ffn2_w_down[i])
    return rms_norm(h, final_norm)
```

```python
import functools
import math

import jax
import jax.numpy as jnp
from jax import lax
from jax.experimental import pallas as pl
from jax.experimental.pallas import tpu as pltpu

F32 = jnp.float32
BF16 = jnp.bfloat16

RMS_EPS = 1e-6
L2_EPS = 1e-6
ROPE_THETA = 10000.0
HEAD_DIM = 128
DA_V_DIM = 2 * HEAD_DIM
CHUNK = 64
CONV_WIDTH = 4
CONV_HALO = 8
N_BETA_DECAY_LANES = 128
VMEM_LIMIT = 56 * 1024 * 1024
MASK_VALUE = -1e30


def _params(semantics):
    return pltpu.CompilerParams(dimension_semantics=semantics, vmem_limit_bytes=VMEM_LIMIT)


def _tile(total, preferred):
    tile = min(preferred, total)
    while total % tile:
        tile //= 2
    return tile


def _rms_normalize(x, w):
    ms = jnp.mean(x * x, axis=-1, keepdims=True)
    return x * lax.rsqrt(ms + RMS_EPS) * w


def _silu(x):
    return x * jax.nn.sigmoid(x)


def _dot(a, b):
    return jnp.dot(a, b, preferred_element_type=F32)


def _dot_nt(a, b):
    return lax.dot_general(a, b, (((1,), (1,)), ((), ())), preferred_element_type=F32)


def _dot_tn(a, b):
    return lax.dot_general(a, b, (((0,), (0,)), ((), ())), preferred_element_type=F32)


def _dot_exact(a, b):
    return jnp.dot(a, b, preferred_element_type=F32, precision=lax.Precision.HIGHEST)


def _dot_nt_exact(a, b):
    return lax.dot_general(a, b, (((1,), (1,)), ((), ())), preferred_element_type=F32,
                           precision=lax.Precision.HIGHEST)


def _ffn_body(h_ref, nw_ref, wg_ref, wu_ref, wd_ref, fw_ref, o_ref, xn_ref, acc_ref, *,
              final_norm):
    j = pl.program_id(1)

    @pl.when(j == 0)
    def _():
        xn_ref[...] = _rms_normalize(h_ref[...], nw_ref[...]).astype(BF16)
        acc_ref[...] = jnp.zeros_like(acc_ref)

    xn = xn_ref[...]
    g = _dot(xn, wg_ref[...])
    u = _dot(xn, wu_ref[...])
    a = (_silu(g) * u).astype(BF16)
    acc_ref[...] += _dot(a, wd_ref[...])

    @pl.when(j == pl.num_programs(1) - 1)
    def _():
        y = h_ref[...] + 0.5 * acc_ref[...]
        if final_norm:
            y = _rms_normalize(y, fw_ref[...])
        o_ref[...] = y


def _ffn(h, norm_w, w_gate, w_up, w_down, final_w=None, *, tm=512, tf=512):
    n, d = h.shape
    f = w_gate.shape[1]
    tm, tf = _tile(n, tm), _tile(f, tf)
    final_norm = final_w is not None
    fw = final_w if final_norm else norm_w
    return pl.pallas_call(
        functools.partial(_ffn_body, final_norm=final_norm),
        out_shape=jax.ShapeDtypeStruct((n, d), F32),
        grid=(n // tm, f // tf),
        in_specs=[
            pl.BlockSpec((tm, d), lambda i, j: (i, 0)),
            pl.BlockSpec((1, d), lambda i, j: (0, 0)),
            pl.BlockSpec((d, tf), lambda i, j: (0, j)),
            pl.BlockSpec((d, tf), lambda i, j: (0, j)),
            pl.BlockSpec((tf, d), lambda i, j: (j, 0)),
            pl.BlockSpec((1, d), lambda i, j: (0, 0)),
        ],
        out_specs=pl.BlockSpec((tm, d), lambda i, j: (i, 0)),
        scratch_shapes=[pltpu.VMEM((tm, d), BF16), pltpu.VMEM((tm, d), F32)],
        compiler_params=_params(("parallel", "arbitrary")),
        name="ffn",
    )(h, norm_w.reshape(1, d), w_gate, w_up, w_down, fw.reshape(1, d))


def _proj_body(h_ref, nw_ref, w_ref, o_ref, xn_ref):
    @pl.when(pl.program_id(1) == 0)
    def _():
        xn_ref[...] = _rms_normalize(h_ref[...], nw_ref[...]).astype(BF16)

    o_ref[...] = _dot(xn_ref[...], w_ref[...]).astype(o_ref.dtype)


def _proj_rope_body(h_ref, nw_ref, w_ref, cos_ref, sin_ref, o_ref, xn_ref, *,
                    n_q_tiles, n_rope_tiles, q_scale):
    j = pl.program_id(1)

    @pl.when(j == 0)
    def _():
        xn_ref[...] = _rms_normalize(h_ref[...], nw_ref[...]).astype(BF16)

    y = _dot(xn_ref[...], w_ref[...])
    tn = y.shape[1]

    @pl.when(j < n_rope_tiles)
    def _():
        cos = cos_ref[...]
        sin = sin_ref[...]
        scale = jnp.where(j < n_q_tiles, q_scale, 1.0).astype(F32)
        for c in range(tn // HEAD_DIM):
            x = y[:, c * HEAD_DIM:(c + 1) * HEAD_DIM]
            swapped = pltpu.roll(x, HEAD_DIM // 2, 1)
            o_ref[:, c * HEAD_DIM:(c + 1) * HEAD_DIM] = (
                (x * cos + swapped * sin) * scale).astype(o_ref.dtype)

    @pl.when(j >= n_rope_tiles)
    def _():
        o_ref[...] = y.astype(o_ref.dtype)


def _norm_proj(h, norm_w, w, out_dtype, *, tm=1024, tn=1024, rope=None):
    n, d = h.shape
    n_out = w.shape[1]
    if rope is None:
        tm, tn = _tile(n, tm), _tile(n_out, tn)
    else:
        tm, tn = _tile(rope[2], tm), _tile(rope[3], tn)
    assert n % tm == 0 and n_out % tn == 0
    in_specs = [
        pl.BlockSpec((tm, d), lambda i, j: (i, 0)),
        pl.BlockSpec((1, d), lambda i, j: (0, 0)),
        pl.BlockSpec((d, tn), lambda i, j: (0, j)),
    ]
    args = [h, norm_w.reshape(1, d), w]
    if rope is None:
        body = _proj_body
    else:
        cos, sin, seq, n_q_cols, n_rope_cols, q_scale = rope
        assert seq % tm == 0 and n_q_cols % tn == 0 and n_rope_cols % tn == 0
        tiles_per_seq = seq // tm
        in_specs += [
            pl.BlockSpec((tm, HEAD_DIM), lambda i, j: (i % tiles_per_seq, 0)),
            pl.BlockSpec((tm, HEAD_DIM), lambda i, j: (i % tiles_per_seq, 0)),
        ]
        args += [cos, sin]
        body = functools.partial(_proj_rope_body, n_q_tiles=n_q_cols // tn,
                                 n_rope_tiles=n_rope_cols // tn, q_scale=q_scale)
    return pl.pallas_call(
        body,
        out_shape=jax.ShapeDtypeStruct((n, n_out), out_dtype),
        grid=(n // tm, n_out // tn),
        in_specs=in_specs,
        out_specs=pl.BlockSpec((tm, tn), lambda i, j: (i, j)),
        scratch_shapes=[pltpu.VMEM((tm, d), BF16)],
        compiler_params=_params(("parallel", "arbitrary")),
        name="norm_proj" if rope is None else "norm_proj_rope",
    )(*args)


def _out_proj_body(x_ref, w_ref, r_ref, o_ref):
    o_ref[...] = r_ref[...] + _dot(x_ref[...], w_ref[...])


def _out_proj(x, w, res, *, tm=512, tn=1024):
    n, k = x.shape
    d = w.shape[1]
    tm, tn = _tile(n, tm), _tile(d, tn)
    return pl.pallas_call(
        _out_proj_body,
        out_shape=jax.ShapeDtypeStruct((n, d), F32),
        grid=(n // tm, d // tn),
        in_specs=[
            pl.BlockSpec((tm, k), lambda i, j: (i, 0)),
            pl.BlockSpec((k, tn), lambda i, j: (0, j)),
            pl.BlockSpec((tm, tn), lambda i, j: (i, j)),
        ],
        out_specs=pl.BlockSpec((tm, tn), lambda i, j: (i, j)),
        compiler_params=_params(("parallel", "parallel")),
        name="out_proj",
    )(x, w, res)


def _diff_attn_body(lam_ref, subln_ref, q_ref, k_ref, v_ref, o_ref, *, tq, lambda_init):
    i = pl.program_id(2)
    lv = lam_ref[...]
    lam = (jnp.exp(jnp.sum(lv[0:1] * lv[1:2], axis=-1, keepdims=True))
           - jnp.exp(jnp.sum(lv[2:3] * lv[3:4], axis=-1, keepdims=True)) + lambda_init)

    row = lax.broadcasted_iota(jnp.int32, (tq, tq), 0)
    col = lax.broadcasted_iota(jnp.int32, (tq, tq), 1)
    causal = col <= row

    def attend(sub):
        q = q_ref[:, sub * HEAD_DIM:(sub + 1) * HEAD_DIM]

        def step(j, carry, masked):
            m, l, acc = carry
            start = pl.multiple_of(j * tq, tq)
            kj = k_ref[pl.ds(start, tq), sub * HEAD_DIM:(sub + 1) * HEAD_DIM]
            vj = v_ref[pl.ds(start, tq), :]
            s = _dot_nt(q, kj)
            if masked:
                s = jnp.where(causal, s, MASK_VALUE)
            m_new = jnp.maximum(m, jnp.max(s, axis=-1, keepdims=True))
            p = jnp.exp(s - m_new)
            alpha = jnp.exp(m - m_new)
            l = alpha * l + jnp.sum(p, axis=-1, keepdims=True)
            acc = alpha * acc + _dot(p.astype(BF16), vj)
            return m_new, l, acc

        init = (jnp.full((tq, 1), MASK_VALUE, F32), jnp.zeros((tq, 1), F32),
                jnp.zeros((tq, DA_V_DIM), F32))
        carry = lax.fori_loop(0, i, functools.partial(step, masked=False), init)
        m, l, acc = step(i, carry, True)
        return acc / l

    o = attend(0) - lam * attend(1)
    o_ref[...] = (_rms_normalize(o, subln_ref[...]) * (1.0 - lambda_init)).astype(o_ref.dtype)


def _diff_attention(qkv, lam_vecs, subln, batch, seq, lambda_init, *, tq=256):
    n, width = qkv.shape
    d = width // 3
    heads = d // DA_V_DIM
    tq = _tile(seq, tq)
    nq = seq // tq
    return pl.pallas_call(
        functools.partial(_diff_attn_body, tq=tq, lambda_init=lambda_init),
        out_shape=jax.ShapeDtypeStruct((n, d), BF16),
        grid=(batch, heads, nq),
        in_specs=[
            pl.BlockSpec((4, HEAD_DIM), lambda b, h, i: (0, 0)),
            pl.BlockSpec((1, DA_V_DIM), lambda b, h, i: (0, 0)),
            pl.BlockSpec((tq, DA_V_DIM), lambda b, h, i: (b * nq + i, h)),
            pl.BlockSpec((seq, DA_V_DIM), lambda b, h, i: (b, heads + h)),
            pl.BlockSpec((seq, DA_V_DIM), lambda b, h, i: (b, 2 * heads + h)),
        ],
        out_specs=pl.BlockSpec((tq, DA_V_DIM), lambda b, h, i: (b * nq + i, h)),
        compiler_params=_params(("parallel", "parallel", "arbitrary")),
        name="diff_attn",
    )(lam_vecs, subln.reshape(1, DA_V_DIM), qkv, qkv, qkv)


def _beta_decay_body(h_ref, nw_ref, w_ref, alog_ref, dt_ref, o_ref, *, n_heads):
    tm = h_ref.shape[0]
    xn = _rms_normalize(h_ref[...], nw_ref[...]).astype(BF16)
    y = _dot(xn, w_ref[...])
    a = y + dt_ref[...]
    softplus = jnp.maximum(a, 0.0) + jnp.log1p(jnp.exp(-jnp.abs(a)))
    g = -jnp.exp(alog_ref[...]) * softplus
    row = lax.broadcasted_iota(jnp.int32, (tm, tm), 0)
    col = lax.broadcasted_iota(jnp.int32, (tm, tm), 1)
    same_chunk = (row // CHUNK) == (col // CHUNK)
    prefix = jnp.where(same_chunk & (col <= row), 1.0, 0.0).astype(F32)
    whole = jnp.where(same_chunk, 1.0, 0.0).astype(F32)
    g_cum = _dot_exact(prefix, g)
    g_all = _dot_exact(whole, g)
    lane = lax.broadcasted_iota(jnp.int32, y.shape, 1)
    o_ref[...] = jnp.where(lane < n_heads, jax.nn.sigmoid(y),
                           jnp.where(lane < 2 * n_heads, g_cum,
                                     jnp.where(lane < 3 * n_heads, g_all, 0.0)))


def _beta_decay(h, norm_w, w_b, w_a, a_log, dt_bias, *, tm=256):
    n, d = h.shape
    n_heads = a_log.shape[0]
    lanes = N_BETA_DECAY_LANES
    assert n % tm == 0 and tm % CHUNK == 0 and 3 * n_heads <= lanes
    pad = lanes - 3 * n_heads
    w = jnp.concatenate([w_b, w_a, w_a, jnp.zeros((d, pad), w_a.dtype)], axis=1).astype(BF16)
    zeros = jnp.zeros((n_heads,), F32)
    a_log_l = jnp.concatenate([zeros, a_log, a_log, jnp.zeros((pad,), F32)]).reshape(1, lanes)
    dt_l = jnp.concatenate([zeros, dt_bias, dt_bias, jnp.zeros((pad,), F32)]).reshape(1, lanes)
    return pl.pallas_call(
        functools.partial(_beta_decay_body, n_heads=n_heads),
        out_shape=jax.ShapeDtypeStruct((n, lanes), F32),
        grid=(n // tm,),
        in_specs=[
            pl.BlockSpec((tm, d), lambda i: (i, 0)),
            pl.BlockSpec((1, d), lambda i: (0, 0)),
            pl.BlockSpec((d, lanes), lambda i: (0, 0)),
            pl.BlockSpec((1, lanes), lambda i: (0, 0)),
            pl.BlockSpec((1, lanes), lambda i: (0, 0)),
        ],
        out_specs=pl.BlockSpec((tm, lanes), lambda i: (i, 0)),
        compiler_params=_params(("parallel",)),
        name="beta_decay",
    )(h, norm_w.reshape(1, d), w, a_log_l, dt_l)


def _delta_body(q_ref, k_ref, v_ref, z_ref, cq_ref, ck_ref, cv_ref, bd_ref, nw_ref, o_ref,
                xq_ref, xk_ref, xv_ref, state_ref, *, tb, n_heads, rep):
    t = pl.program_id(2)
    kh = pl.program_id(1)
    halo = CONV_HALO

    @pl.when(t == 0)
    def _():
        xq_ref[0:halo, :] = jnp.zeros((halo, xq_ref.shape[1]), F32)
        xk_ref[0:halo, :] = jnp.zeros((halo, xk_ref.shape[1]), F32)
        xv_ref[0:halo, :] = jnp.zeros((halo, xv_ref.shape[1]), F32)
        state_ref[...] = jnp.zeros_like(state_ref)

    @pl.when(t > 0)
    def _():
        xq_ref[0:halo, :] = xq_ref[tb:tb + halo, :]
        xk_ref[0:halo, :] = xk_ref[tb:tb + halo, :]
        xv_ref[0:halo, :] = xv_ref[tb:tb + halo, :]

    def conv_silu(x_ref, buf_ref, w_ref):
        buf_ref[halo:halo + tb, :] = x_ref[...].astype(F32)
        w = w_ref[...]
        y = None
        for tap in range(CONV_WIDTH):
            off = halo - (CONV_WIDTH - 1) + tap
            term = w[tap:tap + 1, :] * buf_ref[off:off + tb, :]
            y = term if y is None else y + term
        return _silu(y)

    q = conv_silu(q_ref, xq_ref, cq_ref)
    k = conv_silu(k_ref, xk_ref, ck_ref)
    v = conv_silu(v_ref, xv_ref, cv_ref)
    q = q * lax.rsqrt(jnp.sum(q * q, axis=-1, keepdims=True) + L2_EPS) * (HEAD_DIM ** -0.5)
    k = k * lax.rsqrt(jnp.sum(k * k, axis=-1, keepdims=True) + L2_EPS)
    q_b = q.astype(BF16)
    k_b = k.astype(BF16)
    kk = _dot_nt(k_b, k_b)
    qk = _dot_nt(q_b, k_b)

    row = lax.broadcasted_iota(jnp.int32, (tb, tb), 0)
    col = lax.broadcasted_iota(jnp.int32, (tb, tb), 1)
    same_chunk = (row // CHUNK) == (col // CHUNK)
    tril = same_chunk & (col <= row)
    strict = same_chunk & (col < row)
    eye = jnp.where(row == col, 1.0, 0.0).astype(F32)

    bd = bd_ref[...]
    lanes = N_BETA_DECAY_LANES
    sel_src = lax.broadcasted_iota(jnp.int32, (lanes, 3 * HEAD_DIM), 0)
    sel_dst = lax.broadcasted_iota(jnp.int32, (lanes, 3 * HEAD_DIM), 1) // HEAD_DIM
    row_src = lax.broadcasted_iota(jnp.int32, (8, lanes), 1)
    nw = nw_ref[...]

    for e in range(rep):
        hv = kh * rep + e
        sel = jnp.where(sel_src == hv + sel_dst * n_heads, 1.0, 0.0).astype(F32)
        cols = _dot_exact(bd, sel)
        beta_c = cols[:, 0:HEAD_DIM]
        gcum_c = cols[:, HEAD_DIM:2 * HEAD_DIM]
        gall_c = cols[:, 2 * HEAD_DIM:3 * HEAD_DIM]
        sel_row = jnp.where(row_src == hv + n_heads, 1.0, 0.0).astype(F32)
        gcum_r = _dot_nt_exact(sel_row, bd)[0:1, :]

        reps = tb // HEAD_DIM
        gdiff = jnp.concatenate([gcum_c] * reps, axis=-1) - gcum_r
        decay = jnp.where(tril, jnp.exp(jnp.where(tril, gdiff, 0.0)), 0.0)
        beta_w = jnp.concatenate([beta_c] * reps, axis=-1)

        p = jnp.where(strict, -(kk * decay * beta_w), 0.0)
        tinv = eye + p
        p_pow = _dot(p.astype(BF16), p.astype(BF16))
        width = 2
        while 2 * width < CHUNK:
            p_b = p_pow.astype(BF16)
            pt = _dot(jnp.concatenate([p_b, tinv.astype(BF16)], axis=0), p_b)
            p_pow = pt[:tb]
            tinv = tinv + pt[tb:]
            width *= 2
        tinv = tinv + _dot(tinv.astype(BF16), p_pow.astype(BF16))

        exp_g = jnp.exp(gcum_c)
        v_e = v[:, e * HEAD_DIM:(e + 1) * HEAD_DIM]
        rhs = jnp.concatenate([v_e * beta_c, k * (beta_c * exp_g)], axis=-1).astype(BF16)
        uw = _dot(tinv.astype(BF16), rhs)
        u = uw[:, :HEAD_DIM]
        w = uw[:, HEAD_DIM:]
        qk_d = jnp.where(tril, qk * decay, 0.0).astype(BF16)
        q_g = (q * exp_g).astype(BF16)
        k_g = (k * jnp.exp(gall_c - gcum_c)).astype(BF16)
        g_chunk = jnp.exp(gall_c)

        state = state_ref[e]
        outs = []
        for c in range(tb // CHUNK):
            r0 = c * CHUNK
            rows = slice(r0, r0 + CHUNK)
            wq = jnp.concatenate([w[rows].astype(BF16), q_g[rows]], axis=0)
            ws = _dot(wq, state.astype(BF16))
            v_new = u[rows] - ws[:CHUNK]
            v_new_b = v_new.astype(BF16)
            o_c = ws[CHUNK:] + _dot(qk_d[rows, r0:r0 + CHUNK], v_new_b)
            outs.append(o_c)
            state = state * g_chunk[r0:r0 + 1, :] + _dot_tn(k_g[rows], v_new_b)
        state_ref[e] = state

        o = jnp.concatenate(outs, axis=0)
        z = z_ref[:, e * HEAD_DIM:(e + 1) * HEAD_DIM].astype(F32)
        o = o * lax.rsqrt(jnp.mean(o * o, axis=-1, keepdims=True) + RMS_EPS) * nw * _silu(z)
        o_ref[:, e * HEAD_DIM:(e + 1) * HEAD_DIM] = o.astype(o_ref.dtype)


def _gated_delta(proj, conv_w, beta_decay, norm_w, batch, seq, n_k_heads, n_v_heads, *, tb=256):
    n = proj.shape[0]
    rep = n_v_heads // n_k_heads
    vw = rep * HEAD_DIM
    key_dim = n_k_heads * HEAD_DIM
    val_dim = n_v_heads * HEAD_DIM
    assert seq % tb == 0 and tb % CHUNK == 0 and tb % HEAD_DIM == 0
    nt = seq // tb
    k_blk = key_dim // HEAD_DIM
    v_blk = 2 * key_dim // vw
    z_blk = (2 * key_dim + val_dim) // vw
    return pl.pallas_call(
        functools.partial(_delta_body, tb=tb, n_heads=n_v_heads, rep=rep),
        out_shape=jax.ShapeDtypeStruct((n, val_dim), BF16),
        grid=(batch, n_k_heads, nt),
        in_specs=[
            pl.BlockSpec((tb, HEAD_DIM), lambda b, h, t: (b * nt + t, h)),
            pl.BlockSpec((tb, HEAD_DIM), lambda b, h, t: (b * nt + t, k_blk + h)),
            pl.BlockSpec((tb, vw), lambda b, h, t: (b * nt + t, v_blk + h)),
            pl.BlockSpec((tb, vw), lambda b, h, t: (b * nt + t, z_blk + h)),
            pl.BlockSpec((CONV_WIDTH, HEAD_DIM), lambda b, h, t: (0, h)),
            pl.BlockSpec((CONV_WIDTH, HEAD_DIM), lambda b, h, t: (0, k_blk + h)),
            pl.BlockSpec((CONV_WIDTH, vw), lambda b, h, t: (0, v_blk + h)),
            pl.BlockSpec((tb, N_BETA_DECAY_LANES), lambda b, h, t: (b * nt + t, 0)),
            pl.BlockSpec((1, HEAD_DIM), lambda b, h, t: (0, 0)),
        ],
        out_specs=pl.BlockSpec((tb, vw), lambda b, h, t: (b * nt + t, h)),
        scratch_shapes=[
            pltpu.VMEM((tb + CONV_HALO, HEAD_DIM), F32),
            pltpu.VMEM((tb + CONV_HALO, HEAD_DIM), F32),
            pltpu.VMEM((tb + CONV_HALO, vw), F32),
            pltpu.VMEM((rep, HEAD_DIM, HEAD_DIM), F32),
        ],
        compiler_params=_params(("parallel", "parallel", "arbitrary")),
        name="gated_delta",
    )(proj, proj, proj, proj, conv_w, conv_w, conv_w, beta_decay, norm_w.reshape(1, HEAD_DIM))


def _rope_tables(seq):
    inv = 1.0 / (ROPE_THETA ** (jnp.arange(0, HEAD_DIM, 2, dtype=F32) / HEAD_DIM))
    ang = jnp.arange(seq, dtype=F32)[:, None] * inv[None, :]
    ang = jnp.concatenate([ang, ang], axis=-1)
    sign = jnp.where(jnp.arange(HEAD_DIM) < HEAD_DIM // 2, -1.0, 1.0).astype(F32)
    return jnp.cos(ang), jnp.sin(ang) * sign[None, :]


def _diff_attention_mixer(h, norm_w, w_qkv, lq1, lk1, lq2, lk2, subln, w_o, lambda_init,
                          batch, seq):
    d = h.shape[1]
    cos, sin = _rope_tables(seq)
    rope = (cos, sin, seq, d, 2 * d, HEAD_DIM ** -0.5)
    qkv = _norm_proj(h, norm_w, w_qkv.astype(BF16), BF16, rope=rope)
    lam_vecs = jnp.stack([lq1, lk1, lq2, lk2]).astype(F32)
    attn = _diff_attention(qkv, lam_vecs, subln, batch, seq, lambda_init)
    return _out_proj(attn, w_o.astype(BF16), h)


def _gated_deltanet_mixer(h, norm_w, w_in, conv_w, a_log, dt_bias, gdn_norm, w_o, batch, seq):
    d = h.shape[1]
    n_v_heads = a_log.shape[0]
    val_dim = n_v_heads * HEAD_DIM
    conv_dim = conv_w.shape[1]
    key_dim = (conv_dim - val_dim) // 2
    n_k_heads = key_dim // HEAD_DIM
    main = conv_dim + val_dim
    proj = _norm_proj(h, norm_w, w_in[:, :main].astype(BF16), F32)
    bd = _beta_decay(h, norm_w, w_in[:, main:main + n_v_heads], w_in[:, main + n_v_heads:],
                     a_log, dt_bias)
    o = _gated_delta(proj, conv_w, bd, gdn_norm, batch, seq, n_k_heads, n_v_heads)
    return _out_proj(o, w_o.astype(BF16), h)


def kernel(x, ffn1_norm, ffn1_w_gate, ffn1_w_up, ffn1_w_down, mix_norm, ffn2_norm, ffn2_w_gate, ffn2_w_up, ffn2_w_down, da_w_qkv, da_lambda_q1, da_lambda_k1, da_lambda_q2, da_lambda_k2, da_subln, da_w_o, gdn_w_in, gdn_conv_w, gdn_a_log, gdn_dt_bias, gdn_norm, gdn_w_o, final_norm):
    batch, seq, d = x.shape
    depth = ffn1_norm.shape[0]
    h = x.reshape(batch * seq, d)
    for i in range(depth):
        h = _ffn(h, ffn1_norm[i], ffn1_w_gate[i].astype(BF16), ffn1_w_up[i].astype(BF16),
                 ffn1_w_down[i].astype(BF16))
        j = i // 2
        if i % 2 == 0:
            lambda_init = 0.8 - 0.6 * math.exp(-0.3 * i)
            h = _diff_attention_mixer(h, mix_norm[i], da_w_qkv[j], da_lambda_q1[j],
                                      da_lambda_k1[j], da_lambda_q2[j], da_lambda_k2[j],
                                      da_subln[j], da_w_o[j], lambda_init, batch, seq)
        else:
            h = _gated_deltanet_mixer(h, mix_norm[i], gdn_w_in[j], gdn_conv_w[j], gdn_a_log[j],
                                      gdn_dt_bias[j], gdn_norm[j], gdn_w_o[j], batch, seq)
        last = i == depth - 1
        h = _ffn(h, ffn2_norm[i], ffn2_w_gate[i].astype(BF16), ffn2_w_up[i].astype(BF16),
                 ffn2_w_down[i].astype(BF16), final_norm if last else None)
    return h.reshape(batch, seq, d)
```

```python
import functools
import math

import jax
import jax.numpy as jnp
from jax import lax
from jax.experimental import pallas as pl
from jax.experimental.pallas import tpu as pltpu

F32 = jnp.float32
BF16 = jnp.bfloat16

RMS_EPS = 1e-6
L2_EPS = 1e-6
ROPE_THETA = 10000.0
HEAD_DIM = 128
DA_V_DIM = 2 * HEAD_DIM
CHUNK = 64
CONV_WIDTH = 4
CONV_HALO = 8
N_BETA_DECAY_LANES = 128
VMEM_LIMIT = 56 * 1024 * 1024
MASK_VALUE = -1e30


def _params(semantics):
    return pltpu.CompilerParams(dimension_semantics=semantics, vmem_limit_bytes=VMEM_LIMIT)


def _tile(total, preferred):
    tile = min(preferred, total)
    while total % tile:
        tile //= 2
    return tile


def _rms_normalize(x, w):
    ms = jnp.mean(x * x, axis=-1, keepdims=True)
    return x * lax.rsqrt(ms + RMS_EPS) * w


def _silu(x):
    return x * jax.nn.sigmoid(x)


def _dot(a, b):
    return jnp.dot(a, b, preferred_element_type=F32)


def _dot_nt(a, b):
    return lax.dot_general(a, b, (((1,), (1,)), ((), ())), preferred_element_type=F32)


def _dot_tn(a, b):
    return lax.dot_general(a, b, (((0,), (0,)), ((), ())), preferred_element_type=F32)


def _dot_exact(a, b):
    return jnp.dot(a, b, preferred_element_type=F32, precision=lax.Precision.HIGHEST)


def _ffn_body(h_ref, nw_ref, wg_ref, wu_ref, wd_ref, fw_ref, o_ref, xn_ref, *, final_norm):
    j = pl.program_id(1)

    @pl.when(j == 0)
    def _():
        xn_ref[...] = _rms_normalize(h_ref[...], nw_ref[...]).astype(BF16)
        o_ref[...] = jnp.zeros_like(o_ref)

    xn = xn_ref[...]
    g = _dot(xn, wg_ref[...].astype(BF16))
    u = _dot(xn, wu_ref[...].astype(BF16))
    a = (_silu(g) * u).astype(BF16)
    o_ref[...] += _dot(a, wd_ref[...].astype(BF16))

    @pl.when(j == pl.num_programs(1) - 1)
    def _():
        y = h_ref[...] + 0.5 * o_ref[...]
        if final_norm:
            y = _rms_normalize(y, fw_ref[...])
        o_ref[...] = y


def _ffn(h, norm_w, w_gate, w_up, w_down, layer, final_w=None, *, tm=1024, tf=256):
    n, d = h.shape
    f = w_gate.shape[2]
    tm, tf = _tile(n, tm), _tile(f, tf)
    final_norm = final_w is not None
    fw = final_w if final_norm else norm_w
    return pl.pallas_call(
        functools.partial(_ffn_body, final_norm=final_norm),
        out_shape=jax.ShapeDtypeStruct((n, d), F32),
        grid=(n // tm, f // tf),
        in_specs=[
            pl.BlockSpec((tm, d), lambda i, j: (i, 0), pipeline_mode=pl.Buffered(1)),
            pl.BlockSpec((1, d), lambda i, j: (0, 0)),
            pl.BlockSpec((None, d, tf), lambda i, j: (layer, 0, j)),
            pl.BlockSpec((None, d, tf), lambda i, j: (layer, 0, j)),
            pl.BlockSpec((None, tf, d), lambda i, j: (layer, j, 0)),
            pl.BlockSpec((1, d), lambda i, j: (0, 0)),
        ],
        out_specs=pl.BlockSpec((tm, d), lambda i, j: (i, 0)),
        scratch_shapes=[pltpu.VMEM((tm, d), BF16)],
        compiler_params=_params(("parallel", "arbitrary")),
        name="ffn",
    )(h, norm_w.reshape(1, d), w_gate, w_up, w_down, fw.reshape(1, d))


def _proj_body(h_ref, nw_ref, w_ref, o_ref, xn_ref):
    @pl.when(pl.program_id(1) == 0)
    def _():
        xn_ref[...] = _rms_normalize(h_ref[...], nw_ref[...]).astype(BF16)

    o_ref[...] = _dot(xn_ref[...], w_ref[...].astype(BF16)).astype(o_ref.dtype)


def _proj_rope_body(h_ref, nw_ref, w_ref, cos_ref, sin_ref, o_ref, xn_ref, *,
                    n_q_tiles, n_rope_tiles, q_scale):
    j = pl.program_id(1)

    @pl.when(j == 0)
    def _():
        xn_ref[...] = _rms_normalize(h_ref[...], nw_ref[...]).astype(BF16)

    y = _dot(xn_ref[...], w_ref[...].astype(BF16))
    tn = y.shape[1]

    @pl.when(j < n_rope_tiles)
    def _():
        cos = cos_ref[...]
        sin = sin_ref[...]
        scale = jnp.where(j < n_q_tiles, q_scale, 1.0).astype(F32)
        for c in range(tn // HEAD_DIM):
            x = y[:, c * HEAD_DIM:(c + 1) * HEAD_DIM]
            swapped = pltpu.roll(x, HEAD_DIM // 2, 1)
            o_ref[:, c * HEAD_DIM:(c + 1) * HEAD_DIM] = (
                (x * cos + swapped * sin) * scale).astype(o_ref.dtype)

    @pl.when(j >= n_rope_tiles)
    def _():
        o_ref[...] = y.astype(o_ref.dtype)


def _norm_proj(h, norm_w, w, layer, n_out, out_dtype, *, tm=1024, tn=512, rope=None):
    n, d = h.shape
    if rope is None:
        tm, tn = _tile(n, tm), _tile(n_out, tn)
    else:
        tm, tn = _tile(rope[2], tm), _tile(rope[3], tn)
    assert n % tm == 0 and n_out % tn == 0
    in_specs = [
        pl.BlockSpec((tm, d), lambda i, j: (i, 0)),
        pl.BlockSpec((1, d), lambda i, j: (0, 0)),
        pl.BlockSpec((None, d, tn), lambda i, j: (layer, 0, j)),
    ]
    args = [h, norm_w.reshape(1, d), w]
    if rope is None:
        body = _proj_body
    else:
        cos, sin, seq, n_q_cols, n_rope_cols, q_scale = rope
        assert seq % tm == 0 and n_q_cols % tn == 0 and n_rope_cols % tn == 0
        tiles_per_seq = seq // tm
        in_specs += [
            pl.BlockSpec((tm, HEAD_DIM), lambda i, j: (i % tiles_per_seq, 0)),
            pl.BlockSpec((tm, HEAD_DIM), lambda i, j: (i % tiles_per_seq, 0)),
        ]
        args += [cos, sin]
        body = functools.partial(_proj_rope_body, n_q_tiles=n_q_cols // tn,
                                 n_rope_tiles=n_rope_cols // tn, q_scale=q_scale)
    return pl.pallas_call(
        body,
        out_shape=jax.ShapeDtypeStruct((n, n_out), out_dtype),
        grid=(n // tm, n_out // tn),
        in_specs=in_specs,
        out_specs=pl.BlockSpec((tm, tn), lambda i, j: (i, j)),
        scratch_shapes=[pltpu.VMEM((tm, d), BF16)],
        compiler_params=_params(("parallel", "arbitrary")),
        name="norm_proj" if rope is None else "norm_proj_rope",
    )(*args)


def _out_proj_body(x_ref, w_ref, r_ref, o_ref):
    o_ref[...] = r_ref[...] + _dot(x_ref[...], w_ref[...].astype(BF16))


def _out_proj(x, w, layer, res, *, tm=1024, tn=512):
    n, k = x.shape
    d = w.shape[2]
    tm, tn = _tile(n, tm), _tile(d, tn)
    return pl.pallas_call(
        _out_proj_body,
        out_shape=jax.ShapeDtypeStruct((n, d), F32),
        grid=(n // tm, d // tn),
        in_specs=[
            pl.BlockSpec((tm, k), lambda i, j: (i, 0)),
            pl.BlockSpec((None, k, tn), lambda i, j: (layer, 0, j)),
            pl.BlockSpec((tm, tn), lambda i, j: (i, j)),
        ],
        out_specs=pl.BlockSpec((tm, tn), lambda i, j: (i, j)),
        compiler_params=_params(("parallel", "parallel")),
        name="out_proj",
    )(x, w, res)


def _diff_attn_body(lam_ref, subln_ref, q_ref, k_ref, v_ref, o_ref, *, tq, lambda_init):
    i = pl.program_id(2)
    lv = lam_ref[...]
    lam = (jnp.exp(jnp.sum(lv[0:1] * lv[1:2], axis=-1, keepdims=True))
           - jnp.exp(jnp.sum(lv[2:3] * lv[3:4], axis=-1, keepdims=True)) + lambda_init)

    row = lax.broadcasted_iota(jnp.int32, (tq, tq), 0)
    col = lax.broadcasted_iota(jnp.int32, (tq, tq), 1)
    causal = col <= row
    subs = (0, 1)
    qs = [q_ref[:, s * HEAD_DIM:(s + 1) * HEAD_DIM] for s in subs]

    def step(j, carry, masked):
        start = pl.multiple_of(j * tq, tq)
        vj = v_ref[pl.ds(start, tq), :]
        scores = [_dot_nt(qs[s], k_ref[pl.ds(start, tq), s * HEAD_DIM:(s + 1) * HEAD_DIM])
                  for s in subs]
        new = []
        for s in subs:
            m, l, acc = carry[s]
            sc = jnp.where(causal, scores[s], MASK_VALUE) if masked else scores[s]
            m_new = jnp.maximum(m, jnp.max(sc, axis=-1, keepdims=True))
            p = jnp.exp(sc - m_new)
            alpha = jnp.exp(m - m_new)
            l = alpha * l + jnp.sum(p, axis=-1, keepdims=True)
            acc = alpha * acc + _dot(p.astype(BF16), vj)
            new.append((m_new, l, acc))
        return tuple(new)

    init = tuple((jnp.full((tq, 1), MASK_VALUE, F32), jnp.zeros((tq, 1), F32),
                  jnp.zeros((tq, DA_V_DIM), F32)) for _ in subs)
    carry = lax.fori_loop(0, i, functools.partial(step, masked=False), init)
    (_, l0, acc0), (_, l1, acc1) = step(i, carry, True)
    o = acc0 / l0 - lam * (acc1 / l1)
    o_ref[...] = (_rms_normalize(o, subln_ref[...]) * (1.0 - lambda_init)).astype(o_ref.dtype)


def _diff_attention(qkv, lam_vecs, subln, batch, seq, lambda_init, *, tq=512):
    n, width = qkv.shape
    d = width // 3
    heads = d // DA_V_DIM
    tq = _tile(seq, tq)
    nq = seq // tq
    return pl.pallas_call(
        functools.partial(_diff_attn_body, tq=tq, lambda_init=lambda_init),
        out_shape=jax.ShapeDtypeStruct((n, d), BF16),
        grid=(batch, heads, nq),
        in_specs=[
            pl.BlockSpec((4, HEAD_DIM), lambda b, h, i: (0, 0)),
            pl.BlockSpec((1, DA_V_DIM), lambda b, h, i: (0, 0)),
            pl.BlockSpec((tq, DA_V_DIM), lambda b, h, i: (b * nq + i, h)),
            pl.BlockSpec((seq, DA_V_DIM), lambda b, h, i: (b, heads + h)),
            pl.BlockSpec((seq, DA_V_DIM), lambda b, h, i: (b, 2 * heads + h)),
        ],
        out_specs=pl.BlockSpec((tq, DA_V_DIM), lambda b, h, i: (b * nq + i, h)),
        compiler_params=_params(("parallel", "parallel", "arbitrary")),
        name="diff_attn",
    )(lam_vecs, subln.reshape(1, DA_V_DIM), qkv, qkv, qkv)


def _beta_decay_body(h_ref, nw_ref, w_ref, alog_ref, dt_ref, o_ref, ot_ref, *, n_heads):
    tm = h_ref.shape[0]
    xn = _rms_normalize(h_ref[...], nw_ref[...]).astype(BF16)
    y = _dot(xn, w_ref[...])
    a = y + dt_ref[...]
    softplus = jnp.maximum(a, 0.0) + jnp.log1p(jnp.exp(-jnp.abs(a)))
    g = -jnp.exp(alog_ref[...]) * softplus
    row = lax.broadcasted_iota(jnp.int32, (tm, tm), 0)
    col = lax.broadcasted_iota(jnp.int32, (tm, tm), 1)
    same_chunk = (row // CHUNK) == (col // CHUNK)
    prefix = jnp.where(same_chunk & (col <= row), 1.0, 0.0).astype(F32)
    whole = jnp.where(same_chunk, 1.0, 0.0).astype(F32)
    g_cum = _dot_exact(prefix, g)
    g_all = _dot_exact(whole, g)
    lane = lax.broadcasted_iota(jnp.int32, y.shape, 1)
    packed = jnp.where(lane < n_heads, jax.nn.sigmoid(y),
                       jnp.where(lane < 2 * n_heads, g_cum,
                                 jnp.where(lane < 3 * n_heads, g_all, 0.0)))
    o_ref[...] = packed
    ot_ref[...] = packed.T


def _beta_decay(h, norm_w, w_b, w_a, a_log, dt_bias, *, tm=256):
    n, d = h.shape
    n_heads = a_log.shape[0]
    lanes = N_BETA_DECAY_LANES
    assert n % tm == 0 and tm % CHUNK == 0 and 3 * n_heads <= lanes
    pad = lanes - 3 * n_heads
    w = jnp.concatenate([w_b, w_a, w_a, jnp.zeros((d, pad), w_a.dtype)], axis=1).astype(BF16)
    zeros = jnp.zeros((n_heads,), F32)
    a_log_l = jnp.concatenate([zeros, a_log, a_log, jnp.zeros((pad,), F32)]).reshape(1, lanes)
    dt_l = jnp.concatenate([zeros, dt_bias, dt_bias, jnp.zeros((pad,), F32)]).reshape(1, lanes)
    return pl.pallas_call(
        functools.partial(_beta_decay_body, n_heads=n_heads),
        out_shape=(jax.ShapeDtypeStruct((n, lanes), F32), jax.ShapeDtypeStruct((lanes, n), F32)),
        grid=(n // tm,),
        in_specs=[
            pl.BlockSpec((tm, d), lambda i: (i, 0)),
            pl.BlockSpec((1, d), lambda i: (0, 0)),
            pl.BlockSpec((d, lanes), lambda i: (0, 0)),
            pl.BlockSpec((1, lanes), lambda i: (0, 0)),
            pl.BlockSpec((1, lanes), lambda i: (0, 0)),
        ],
        out_specs=(pl.BlockSpec((tm, lanes), lambda i: (i, 0)),
                   pl.BlockSpec((lanes, tm), lambda i: (0, i))),
        compiler_params=_params(("parallel",)),
        name="beta_decay",
    )(h, norm_w.reshape(1, d), w, a_log_l, dt_l)


def _delta_body(q_ref, k_ref, v_ref, z_ref, cq_ref, ck_ref, cv_ref, bd_ref, bdt_ref, nw_ref,
                o_ref, xq_ref, xk_ref, xv_ref, state_ref, *, tb, n_heads, rep, khs):
    t = pl.program_id(2)
    hv0 = pl.program_id(1) * (khs * rep)
    halo = CONV_HALO
    n_v = khs * rep
    n_chunks = tb // CHUNK

    @pl.when(t == 0)
    def _():
        xq_ref[0:halo, :] = jnp.zeros((halo, xq_ref.shape[1]), F32)
        xk_ref[0:halo, :] = jnp.zeros((halo, xk_ref.shape[1]), F32)
        xv_ref[0:halo, :] = jnp.zeros((halo, xv_ref.shape[1]), F32)
        state_ref[...] = jnp.zeros_like(state_ref)

    @pl.when(t > 0)
    def _():
        xq_ref[0:halo, :] = xq_ref[tb:tb + halo, :]
        xk_ref[0:halo, :] = xk_ref[tb:tb + halo, :]
        xv_ref[0:halo, :] = xv_ref[tb:tb + halo, :]

    def conv_silu(x_ref, buf_ref, w_ref):
        buf_ref[halo:halo + tb, :] = x_ref[...].astype(F32)
        w = w_ref[...]
        y = None
        for tap in range(CONV_WIDTH):
            off = halo - (CONV_WIDTH - 1) + tap
            term = w[tap:tap + 1, :] * buf_ref[off:off + tb, :]
            y = term if y is None else y + term
        return _silu(y)

    q_all = conv_silu(q_ref, xq_ref, cq_ref)
    k_all = conv_silu(k_ref, xk_ref, ck_ref)
    v_all = conv_silu(v_ref, xv_ref, cv_ref)

    def head_slice(x, i):
        return x[:, i * HEAD_DIM:(i + 1) * HEAD_DIM]

    qs, ks, kks, qks = [], [], [], []
    for a in range(khs):
        q = head_slice(q_all, a)
        k = head_slice(k_all, a)
        q = q * lax.rsqrt(jnp.sum(q * q, axis=-1, keepdims=True) + L2_EPS) * (HEAD_DIM ** -0.5)
        k = k * lax.rsqrt(jnp.sum(k * k, axis=-1, keepdims=True) + L2_EPS)
        qs.append(q)
        ks.append(k)
        k_b = k.astype(BF16)
        kks.append(_dot_nt(k_b, k_b))
        qks.append(_dot_nt(q.astype(BF16), k_b))

    row = lax.broadcasted_iota(jnp.int32, (tb, tb), 0)
    col = lax.broadcasted_iota(jnp.int32, (tb, tb), 1)
    same_chunk = (row // CHUNK) == (col // CHUNK)
    tril = same_chunk & (col <= row)
    strict = same_chunk & (col < row)

    bd = bd_ref[...]
    lanes = N_BETA_DECAY_LANES
    hi = bd.astype(BF16)
    rest = bd - hi.astype(F32)
    mid = rest.astype(BF16)
    lo = (rest - mid.astype(F32)).astype(BF16)
    pieces = jnp.concatenate([hi, mid, lo], axis=-1)
    sel_w = 3 * HEAD_DIM * n_v
    src = lax.broadcasted_iota(jnp.int32, (3 * lanes, sel_w), 0) % lanes
    dst = lax.broadcasted_iota(jnp.int32, (3 * lanes, sel_w), 1)
    want = hv0 + dst // (3 * HEAD_DIM) + ((dst // HEAD_DIM) % 3) * n_heads
    cols = _dot(pieces, jnp.where(src == want, 1.0, 0.0).astype(BF16))

    reps = tb // HEAD_DIM
    heads = [(a, e) for a in range(khs) for e in range(rep)]
    p_cur, rhs, qk_d, q_g, k_g, g_chunk = [], [], [], [], [], []
    for idx, (a, e) in enumerate(heads):
        base = idx * 3 * HEAD_DIM
        beta_c = cols[:, base:base + HEAD_DIM]
        gcum_c = cols[:, base + HEAD_DIM:base + 2 * HEAD_DIM]
        gall_c = cols[:, base + 2 * HEAD_DIM:base + 3 * HEAD_DIM]
        gcum_r = bdt_ref[pl.ds(n_heads + hv0 + idx, 1), :]
        gdiff = jnp.concatenate([gcum_c] * reps, axis=-1) - gcum_r
        decay = jnp.where(tril, jnp.exp(jnp.where(tril, gdiff, 0.0)), 0.0)
        beta_w = jnp.concatenate([beta_c] * reps, axis=-1)
        p_cur.append(jnp.where(strict, -(kks[a] * decay * beta_w), 0.0).astype(BF16))
        exp_g = jnp.exp(gcum_c)
        rhs.append(jnp.concatenate([head_slice(v_all, idx) * beta_c,
                                    ks[a] * (beta_c * exp_g)], axis=-1))
        qk_d.append(jnp.where(tril, qks[a] * decay, 0.0).astype(BF16))
        q_g.append((qs[a] * exp_g).astype(BF16))
        k_g.append((ks[a] * jnp.exp(gall_c - gcum_c)).astype(BF16))
        g_chunk.append(jnp.exp(gall_c))

    n_factors = CHUNK.bit_length() - 1
    for j in range(n_factors):
        for idx in range(n_v):
            p_b = p_cur[idx]
            r_b = rhs[idx].astype(BF16)
            if j < n_factors - 1:
                res = _dot(p_b, jnp.concatenate([p_b, r_b], axis=-1))
                p_cur[idx] = res[:, :tb].astype(BF16)
                rhs[idx] = rhs[idx] + res[:, tb:]
            else:
                rhs[idx] = rhs[idx] + _dot(p_b, r_b)

    states = [state_ref[idx] for idx in range(n_v)]
    outs = [[] for _ in range(n_v)]
    for c in range(n_chunks):
        r0 = c * CHUNK
        rows = slice(r0, r0 + CHUNK)
        for idx in range(n_v):
            u = rhs[idx][rows, :HEAD_DIM]
            w = rhs[idx][rows, HEAD_DIM:]
            wq = jnp.concatenate([w.astype(BF16), q_g[idx][rows]], axis=0)
            ws = _dot(wq, states[idx].astype(BF16))
            v_new_b = (u - ws[:CHUNK]).astype(BF16)
            outs[idx].append(ws[CHUNK:] + _dot(qk_d[idx][rows, r0:r0 + CHUNK], v_new_b))
            states[idx] = (states[idx] * g_chunk[idx][r0:r0 + 1, :]
                           + _dot_tn(k_g[idx][rows], v_new_b))

    nw = nw_ref[...]
    for idx in range(n_v):
        state_ref[idx] = states[idx]
        o = jnp.concatenate(outs[idx], axis=0)
        z = head_slice(z_ref, idx).astype(F32)
        o = o * lax.rsqrt(jnp.mean(o * o, axis=-1, keepdims=True) + RMS_EPS) * nw * _silu(z)
        o_ref[:, idx * HEAD_DIM:(idx + 1) * HEAD_DIM] = o.astype(o_ref.dtype)


def _gated_delta(proj, conv_w, layer, bd, bd_t, norm_w, batch, seq, n_k_heads, n_v_heads, *,
                 tb=256, khs=2):
    n = proj.shape[0]
    rep = n_v_heads // n_k_heads
    khs = _tile(n_k_heads, khs)
    kw = khs * HEAD_DIM
    vw = khs * rep * HEAD_DIM
    key_dim = n_k_heads * HEAD_DIM
    val_dim = n_v_heads * HEAD_DIM
    assert seq % tb == 0 and tb % CHUNK == 0 and tb % HEAD_DIM == 0
    nt = seq // tb
    k_blk = key_dim // kw
    v_blk = 2 * key_dim // vw
    z_blk = (2 * key_dim + val_dim) // vw
    lanes = N_BETA_DECAY_LANES
    return pl.pallas_call(
        functools.partial(_delta_body, tb=tb, n_heads=n_v_heads, rep=rep, khs=khs),
        out_shape=jax.ShapeDtypeStruct((n, val_dim), BF16),
        grid=(batch, n_k_heads // khs, nt),
        in_specs=[
            pl.BlockSpec((tb, kw), lambda b, h, t: (b * nt + t, h)),
            pl.BlockSpec((tb, kw), lambda b, h, t: (b * nt + t, k_blk + h)),
            pl.BlockSpec((tb, vw), lambda b, h, t: (b * nt + t, v_blk + h)),
            pl.BlockSpec((tb, vw), lambda b, h, t: (b * nt + t, z_blk + h)),
            pl.BlockSpec((None, CONV_WIDTH, kw), lambda b, h, t: (layer, 0, h)),
            pl.BlockSpec((None, CONV_WIDTH, kw), lambda b, h, t: (layer, 0, k_blk + h)),
            pl.BlockSpec((None, CONV_WIDTH, vw), lambda b, h, t: (layer, 0, v_blk + h)),
            pl.BlockSpec((tb, lanes), lambda b, h, t: (b * nt + t, 0)),
            pl.BlockSpec((lanes, tb), lambda b, h, t: (0, b * nt + t)),
            pl.BlockSpec((1, HEAD_DIM), lambda b, h, t: (0, 0)),
        ],
        out_specs=pl.BlockSpec((tb, vw), lambda b, h, t: (b * nt + t, h)),
        scratch_shapes=[
            pltpu.VMEM((tb + CONV_HALO, kw), F32),
            pltpu.VMEM((tb + CONV_HALO, kw), F32),
            pltpu.VMEM((tb + CONV_HALO, vw), F32),
            pltpu.VMEM((khs * rep, HEAD_DIM, HEAD_DIM), F32),
        ],
        compiler_params=_params(("parallel", "parallel", "arbitrary")),
        name="gated_delta",
    )(proj, proj, proj, proj, conv_w, conv_w, conv_w, bd, bd_t, norm_w.reshape(1, HEAD_DIM))


def _rope_tables(seq):
    inv = 1.0 / (ROPE_THETA ** (jnp.arange(0, HEAD_DIM, 2, dtype=F32) / HEAD_DIM))
    ang = jnp.arange(seq, dtype=F32)[:, None] * inv[None, :]
    ang = jnp.concatenate([ang, ang], axis=-1)
    sign = jnp.where(jnp.arange(HEAD_DIM) < HEAD_DIM // 2, -1.0, 1.0).astype(F32)
    return jnp.cos(ang), jnp.sin(ang) * sign[None, :]


def _diff_attention_mixer(h, norm_w, w_qkv, layer, lq1, lk1, lq2, lk2, subln, w_o,
                          lambda_init, batch, seq):
    d = h.shape[1]
    cos, sin = _rope_tables(seq)
    rope = (cos, sin, seq, d, 2 * d, HEAD_DIM ** -0.5)
    qkv = _norm_proj(h, norm_w, w_qkv, layer, 3 * d, BF16, rope=rope)
    lam_vecs = jnp.stack([lq1, lk1, lq2, lk2]).astype(F32)
    attn = _diff_attention(qkv, lam_vecs, subln, batch, seq, lambda_init)
    return _out_proj(attn, w_o, layer, h)


def _gated_deltanet_mixer(h, norm_w, w_in, conv_w, layer, a_log, dt_bias, gdn_norm, w_o,
                          batch, seq):
    n_v_heads = a_log.shape[0]
    val_dim = n_v_heads * HEAD_DIM
    conv_dim = conv_w.shape[2]
    key_dim = (conv_dim - val_dim) // 2
    n_k_heads = key_dim // HEAD_DIM
    main = conv_dim + val_dim
    proj = _norm_proj(h, norm_w, w_in, layer, main, BF16)
    bd, bd_t = _beta_decay(h, norm_w, w_in[layer, :, main:main + n_v_heads],
                           w_in[layer, :, main + n_v_heads:], a_log, dt_bias)
    o = _gated_delta(proj, conv_w, layer, bd, bd_t, gdn_norm, batch, seq, n_k_heads,
                     n_v_heads)
    return _out_proj(o, w_o, layer, h)


def kernel(x, ffn1_norm, ffn1_w_gate, ffn1_w_up, ffn1_w_down, mix_norm, ffn2_norm, ffn2_w_gate, ffn2_w_up, ffn2_w_down, da_w_qkv, da_lambda_q1, da_lambda_k1, da_lambda_q2, da_lambda_k2, da_subln, da_w_o, gdn_w_in, gdn_conv_w, gdn_a_log, gdn_dt_bias, gdn_norm, gdn_w_o, final_norm):
    batch, seq, d = x.shape
    depth = ffn1_norm.shape[0]
    h = x.reshape(batch * seq, d)
    for i in range(depth):
        h = _ffn(h, ffn1_norm[i], ffn1_w_gate, ffn1_w_up, ffn1_w_down, i)
        j = i // 2
        if i % 2 == 0:
            lambda_init = 0.8 - 0.6 * math.exp(-0.3 * i)
            h = _diff_attention_mixer(h, mix_norm[i], da_w_qkv, j, da_lambda_q1[j],
                                      da_lambda_k1[j], da_lambda_q2[j], da_lambda_k2[j],
                                      da_subln[j], da_w_o, lambda_init, batch, seq)
        else:
            h = _gated_deltanet_mixer(h, mix_norm[i], gdn_w_in, gdn_conv_w, j, gdn_a_log[j],
                                      gdn_dt_bias[j], gdn_norm[j], gdn_w_o, batch, seq)
        last = i == depth - 1
        h = _ffn(h, ffn2_norm[i], ffn2_w_gate, ffn2_w_up, ffn2_w_down, i,
                 final_norm if last else None)
    return h.reshape(batch, seq, d)
```

```python
import functools
import math

import jax
import jax.numpy as jnp
from jax import lax
from jax.experimental import pallas as pl
from jax.experimental.pallas import tpu as pltpu

F32 = jnp.float32
BF16 = jnp.bfloat16

RMS_EPS = 1e-6
L2_EPS = 1e-6
ROPE_THETA = 10000.0
HEAD_DIM = 128
DA_V_DIM = 2 * HEAD_DIM
CHUNK = 64
CONV_WIDTH = 4
CONV_HALO = 16
MXU_COLS = 256
N_BETA_DECAY_LANES = 128
VMEM_LIMIT = 56 * 1024 * 1024
MASK_VALUE = -1e30


def _params(semantics):
    return pltpu.CompilerParams(dimension_semantics=semantics, vmem_limit_bytes=VMEM_LIMIT)


def _tile(total, preferred):
    tile = min(preferred, total)
    while total % tile:
        tile //= 2
    return tile


def _rms_normalize(x, w):
    ms = jnp.mean(x * x, axis=-1, keepdims=True)
    return x * lax.rsqrt(ms + RMS_EPS) * w


def _silu(x):
    return x * jax.nn.sigmoid(x)


def _dot(a, b):
    return jnp.dot(a, b, preferred_element_type=F32)


def _dot_nt(a, b):
    return lax.dot_general(a, b, (((1,), (1,)), ((), ())), preferred_element_type=F32)


def _dot_tn(a, b):
    return lax.dot_general(a, b, (((0,), (0,)), ((), ())), preferred_element_type=F32)


def _dot_exact(a, b):
    return jnp.dot(a, b, preferred_element_type=F32, precision=lax.Precision.HIGHEST)


def _ffn_body(h_ref, nw_ref, wg_ref, wu_ref, wd_ref, ew_ref, o_ref, xn_ref, *, epilogue):
    j = pl.program_id(1)

    @pl.when(j == 0)
    def _():
        xn_ref[...] = _rms_normalize(h_ref[...], nw_ref[...]).astype(BF16)
        o_ref[...] = jnp.zeros_like(o_ref)

    xn = xn_ref[...]
    g = _dot(xn, wg_ref[...].astype(BF16))
    u = _dot(xn, wu_ref[...].astype(BF16))
    a = (_silu(g) * u).astype(BF16)
    o_ref[...] += _dot(a, wd_ref[...].astype(BF16))

    @pl.when(j == pl.num_programs(1) - 1)
    def _():
        y = h_ref[...] + 0.5 * o_ref[...]
        if epilogue == "final":
            o_ref[...] = _rms_normalize(y, ew_ref[...])
        else:
            o_ref[...] = y
            if epilogue == "next":
                xn_ref[...] = _rms_normalize(y, ew_ref[...]).astype(BF16)


def _ffn(h, norm_w, w_gate, w_up, w_down, layer, epilogue=None, epilogue_w=None, *,
         tm=1024, tf=256):
    n, d = h.shape
    f = w_gate.shape[2]
    tm, tf = _tile(n, tm), _tile(f, tf)
    ew = norm_w if epilogue_w is None else epilogue_w
    row_block = pl.BlockSpec((tm, d), lambda i, j: (i, 0))
    out_shape = jax.ShapeDtypeStruct((n, d), F32)
    out_specs = row_block
    scratch_shapes = [pltpu.VMEM((tm, d), BF16)]
    if epilogue == "next":
        out_shape = (out_shape, jax.ShapeDtypeStruct((n, d), BF16))
        out_specs = (row_block, row_block)
        scratch_shapes = []
    return pl.pallas_call(
        functools.partial(_ffn_body, epilogue=epilogue),
        out_shape=out_shape,
        grid=(n // tm, f // tf),
        in_specs=[
            pl.BlockSpec((tm, d), lambda i, j: (i, 0), pipeline_mode=pl.Buffered(1)),
            pl.BlockSpec((1, d), lambda i, j: (0, 0)),
            pl.BlockSpec((None, d, tf), lambda i, j: (layer, 0, j)),
            pl.BlockSpec((None, d, tf), lambda i, j: (layer, 0, j)),
            pl.BlockSpec((None, tf, d), lambda i, j: (layer, j, 0)),
            pl.BlockSpec((1, d), lambda i, j: (0, 0)),
        ],
        out_specs=out_specs,
        scratch_shapes=scratch_shapes,
        compiler_params=_params(("parallel", "arbitrary")),
        name="ffn",
    )(h, norm_w.reshape(1, d), w_gate, w_up, w_down, ew.reshape(1, d))


def _proj_rope_body(x_ref, w_ref, cos_ref, sin_ref, o_ref, wb_ref, *,
                    n_q_tiles, n_rope_tiles, q_scale):
    j = pl.program_id(0)

    @pl.when(pl.program_id(1) == 0)
    def _():
        wb_ref[...] = w_ref[...].astype(BF16)

    tn = o_ref.shape[1]

    @pl.when(j < n_rope_tiles)
    def _():
        cos = cos_ref[...]
        sin = sin_ref[...]
        scale = jnp.where(j < n_q_tiles, q_scale, 1.0).astype(F32)
        for g in range(tn // MXU_COLS):
            y = _dot(x_ref[...], wb_ref[:, g * MXU_COLS:(g + 1) * MXU_COLS])
            for c in range(MXU_COLS // HEAD_DIM):
                x = y[:, c * HEAD_DIM:(c + 1) * HEAD_DIM]
                swapped = pltpu.roll(x, HEAD_DIM // 2, 1)
                lo = g * MXU_COLS + c * HEAD_DIM
                o_ref[:, lo:lo + HEAD_DIM] = (
                    (x * cos + swapped * sin) * scale).astype(o_ref.dtype)

    @pl.when(j >= n_rope_tiles)
    def _():
        o_ref[...] = _dot(x_ref[...], wb_ref[...]).astype(o_ref.dtype)


def _proj_rope(x, w, layer, cos, sin, seq, n_q_cols, n_rope_cols, q_scale, *, tm=1024, tn=1024):
    n, d = x.shape
    n_out = w.shape[2]
    tm, tn = _tile(seq, tm), _tile(n_q_cols, tn)
    assert n % tm == 0 and n_out % tn == 0 and n_rope_cols % tn == 0
    tiles_per_seq = seq // tm
    return pl.pallas_call(
        functools.partial(_proj_rope_body, n_q_tiles=n_q_cols // tn,
                          n_rope_tiles=n_rope_cols // tn, q_scale=q_scale),
        out_shape=jax.ShapeDtypeStruct((n, n_out), BF16),
        grid=(n_out // tn, n // tm),
        in_specs=[
            pl.BlockSpec((tm, d), lambda j, i: (i, 0)),
            pl.BlockSpec((None, d, tn), lambda j, i: (layer, 0, j)),
            pl.BlockSpec((tm, HEAD_DIM), lambda j, i: (i % tiles_per_seq, 0)),
            pl.BlockSpec((tm, HEAD_DIM), lambda j, i: (i % tiles_per_seq, 0)),
        ],
        out_specs=pl.BlockSpec((tm, tn), lambda j, i: (i, j)),
        scratch_shapes=[pltpu.VMEM((d, tn), BF16)],
        compiler_params=_params(("parallel", "arbitrary")),
        name="proj_rope",
    )(x, w, cos, sin)


def _proj_conv_body(x_ref, xh_ref, w_ref, cw_ref, o_ref, wb_ref, *, n_conv_tiles, tiles_per_seq):
    j = pl.program_id(0)
    i = pl.program_id(1)
    tm = x_ref.shape[0]

    @pl.when(i == 0)
    def _():
        wb_ref[...] = w_ref[...].astype(BF16)

    keep = jnp.where(i % tiles_per_seq == 0, 0.0, 1.0).astype(BF16)
    @pl.when(j < n_conv_tiles)
    def _():
        lhs = jnp.concatenate([xh_ref[...] * keep, x_ref[...]], axis=0)
        for g in range(o_ref.shape[1] // MXU_COLS):
            cols = slice(g * MXU_COLS, (g + 1) * MXU_COLS)
            y = _dot(lhs, wb_ref[:, cols])
            cw = cw_ref[:, cols]
            acc = None
            for tap in range(CONV_WIDTH):
                off = CONV_HALO - (CONV_WIDTH - 1) + tap
                term = cw[tap:tap + 1, :] * y[off:off + tm, :]
                acc = term if acc is None else acc + term
            o_ref[:, cols] = _silu(acc).astype(o_ref.dtype)

    @pl.when(j >= n_conv_tiles)
    def _():
        o_ref[...] = _dot(x_ref[...], wb_ref[...]).astype(o_ref.dtype)


def _proj_conv(x, w, conv_w, layer, seq, n_out, *, tm=1024, tn=1024):
    n, d = x.shape
    n_conv = conv_w.shape[2]
    tm, tn = _tile(seq, tm), _tile(math.gcd(n_conv, n_out), tn)
    assert n % tm == 0 and tm % CONV_HALO == 0
    tiles_per_seq = seq // tm
    n_conv_tiles = n_conv // tn
    halo_blocks = tm // CONV_HALO
    return pl.pallas_call(
        functools.partial(_proj_conv_body, n_conv_tiles=n_conv_tiles,
                          tiles_per_seq=tiles_per_seq),
        out_shape=jax.ShapeDtypeStruct((n, n_out), BF16),
        grid=(n_out // tn, n // tm),
        in_specs=[
            pl.BlockSpec((tm, d), lambda j, i: (i, 0)),
            pl.BlockSpec((CONV_HALO, d), lambda j, i: (jnp.maximum(i * halo_blocks - 1, 0), 0)),
            pl.BlockSpec((None, d, tn), lambda j, i: (layer, 0, j)),
            pl.BlockSpec((None, CONV_WIDTH, tn),
                         lambda j, i: (layer, 0, jnp.minimum(j, n_conv_tiles - 1))),
        ],
        out_specs=pl.BlockSpec((tm, tn), lambda j, i: (i, j)),
        scratch_shapes=[pltpu.VMEM((d, tn), BF16)],
        compiler_params=_params(("parallel", "arbitrary")),
        name="proj_conv",
    )(x, x, w, conv_w)


def _out_proj_body(x_ref, w_ref, r_ref, o_ref, wb_ref):
    @pl.when(pl.program_id(1) == 0)
    def _():
        wb_ref[...] = w_ref[...].astype(BF16)

    o_ref[...] = r_ref[...] + _dot(x_ref[...], wb_ref[...])


def _out_proj(x, w, layer, res, *, tm=512, tn=512):
    n, k = x.shape
    d = w.shape[2]
    tm, tn = _tile(n, tm), _tile(d, tn)
    return pl.pallas_call(
        _out_proj_body,
        out_shape=jax.ShapeDtypeStruct((n, d), F32),
        grid=(d // tn, n // tm),
        in_specs=[
            pl.BlockSpec((tm, k), lambda j, i: (i, 0)),
            pl.BlockSpec((None, k, tn), lambda j, i: (layer, 0, j)),
            pl.BlockSpec((tm, tn), lambda j, i: (i, j)),
        ],
        out_specs=pl.BlockSpec((tm, tn), lambda j, i: (i, j)),
        scratch_shapes=[pltpu.VMEM((k, tn), BF16)],
        compiler_params=_params(("parallel", "arbitrary")),
        name="out_proj",
    )(x, w, res)


def _diff_attn_body(lam_ref, subln_ref, q_ref, k_ref, v_ref, o_ref, *, tq, lambda_init):
    i = pl.program_id(2)
    lv = lam_ref[...]
    lam = (jnp.exp(jnp.sum(lv[0:1] * lv[1:2], axis=-1, keepdims=True))
           - jnp.exp(jnp.sum(lv[2:3] * lv[3:4], axis=-1, keepdims=True)) + lambda_init)

    row = lax.broadcasted_iota(jnp.int32, (tq, tq), 0)
    col = lax.broadcasted_iota(jnp.int32, (tq, tq), 1)
    causal = col <= row
    subs = (0, 1)
    qs = [q_ref[:, s * HEAD_DIM:(s + 1) * HEAD_DIM] for s in subs]

    def step(j, carry, masked):
        start = pl.multiple_of(j * tq, tq)
        vj = v_ref[pl.ds(start, tq), :]
        scores = [_dot_nt(qs[s], k_ref[pl.ds(start, tq), s * HEAD_DIM:(s + 1) * HEAD_DIM])
                  for s in subs]
        new = []
        for s in subs:
            m, l, acc = carry[s]
            sc = jnp.where(causal, scores[s], MASK_VALUE) if masked else scores[s]
            m_new = jnp.maximum(m, jnp.max(sc, axis=-1, keepdims=True))
            p = jnp.exp(sc - m_new)
            alpha = jnp.exp(m - m_new)
            l = alpha * l + jnp.sum(p, axis=-1, keepdims=True)
            acc = alpha * acc + _dot(p.astype(BF16), vj)
            new.append((m_new, l, acc))
        return tuple(new)

    init = tuple((jnp.full((tq, 1), MASK_VALUE, F32), jnp.zeros((tq, 1), F32),
                  jnp.zeros((tq, DA_V_DIM), F32)) for _ in subs)
    carry = lax.fori_loop(0, i, functools.partial(step, masked=False), init)
    (_, l0, acc0), (_, l1, acc1) = step(i, carry, True)
    o = acc0 / l0 - lam * (acc1 / l1)
    o_ref[...] = (_rms_normalize(o, subln_ref[...]) * (1.0 - lambda_init)).astype(o_ref.dtype)


def _diff_attention(qkv, lam_vecs, subln, batch, seq, lambda_init, *, tq=512):
    n, width = qkv.shape
    d = width // 3
    heads = d // DA_V_DIM
    tq = _tile(seq, tq)
    nq = seq // tq
    return pl.pallas_call(
        functools.partial(_diff_attn_body, tq=tq, lambda_init=lambda_init),
        out_shape=jax.ShapeDtypeStruct((n, d), BF16),
        grid=(batch, heads, nq),
        in_specs=[
            pl.BlockSpec((4, HEAD_DIM), lambda b, h, i: (0, 0)),
            pl.BlockSpec((1, DA_V_DIM), lambda b, h, i: (0, 0)),
            pl.BlockSpec((tq, DA_V_DIM), lambda b, h, i: (b * nq + i, h)),
            pl.BlockSpec((seq, DA_V_DIM), lambda b, h, i: (b, heads + h)),
            pl.BlockSpec((seq, DA_V_DIM), lambda b, h, i: (b, 2 * heads + h)),
        ],
        out_specs=pl.BlockSpec((tq, DA_V_DIM), lambda b, h, i: (b * nq + i, h)),
        compiler_params=_params(("parallel", "parallel", "arbitrary")),
        name="diff_attn",
    )(lam_vecs, subln.reshape(1, DA_V_DIM), qkv, qkv, qkv)


def _beta_decay_body(x_ref, w_ref, alog_ref, dt_ref, o_ref, ot_ref, *, n_heads):
    tm = x_ref.shape[0]
    y = _dot(x_ref[...], w_ref[...])
    a = y + dt_ref[...]
    softplus = jnp.maximum(a, 0.0) + jnp.log1p(jnp.exp(-jnp.abs(a)))
    g = -jnp.exp(alog_ref[...]) * softplus
    row = lax.broadcasted_iota(jnp.int32, (tm, tm), 0)
    col = lax.broadcasted_iota(jnp.int32, (tm, tm), 1)
    same_chunk = (row // CHUNK) == (col // CHUNK)
    prefix = jnp.where(same_chunk & (col <= row), 1.0, 0.0).astype(F32)
    whole = jnp.where(same_chunk, 1.0, 0.0).astype(F32)
    g_cum = _dot_exact(prefix, g)
    g_all = _dot_exact(whole, g)
    lane = lax.broadcasted_iota(jnp.int32, y.shape, 1)
    packed = jnp.where(lane < n_heads, jax.nn.sigmoid(y),
                       jnp.where(lane < 2 * n_heads, g_cum,
                                 jnp.where(lane < 3 * n_heads, g_all, 0.0)))
    o_ref[...] = packed
    ot_ref[...] = packed.T


def _beta_decay(x, w_b, w_a, a_log, dt_bias, *, tm=256):
    n, d = x.shape
    n_heads = a_log.shape[0]
    lanes = N_BETA_DECAY_LANES
    assert n % tm == 0 and tm % CHUNK == 0 and 3 * n_heads <= lanes
    pad = lanes - 3 * n_heads
    w = jnp.concatenate([w_b, w_a, w_a, jnp.zeros((d, pad), w_a.dtype)], axis=1).astype(BF16)
    zeros = jnp.zeros((n_heads,), F32)
    a_log_l = jnp.concatenate([zeros, a_log, a_log, jnp.zeros((pad,), F32)]).reshape(1, lanes)
    dt_l = jnp.concatenate([zeros, dt_bias, dt_bias, jnp.zeros((pad,), F32)]).reshape(1, lanes)
    return pl.pallas_call(
        functools.partial(_beta_decay_body, n_heads=n_heads),
        out_shape=(jax.ShapeDtypeStruct((n, lanes), F32), jax.ShapeDtypeStruct((lanes, n), F32)),
        grid=(n // tm,),
        in_specs=[
            pl.BlockSpec((tm, d), lambda i: (i, 0)),
            pl.BlockSpec((d, lanes), lambda i: (0, 0)),
            pl.BlockSpec((1, lanes), lambda i: (0, 0)),
            pl.BlockSpec((1, lanes), lambda i: (0, 0)),
        ],
        out_specs=(pl.BlockSpec((tm, lanes), lambda i: (i, 0)),
                   pl.BlockSpec((lanes, tm), lambda i: (0, i))),
        compiler_params=_params(("parallel",)),
        name="beta_decay",
    )(x, w, a_log_l, dt_l)


def _delta_body(q_ref, k_ref, v_ref, z_ref, bd_ref, bdt_ref, nw_ref, o_ref, state_ref, *,
                tb, n_heads, rep, khs):
    t = pl.program_id(2)
    hv0 = pl.program_id(1) * (khs * rep)
    n_v = khs * rep
    n_chunks = tb // CHUNK
    lanes = N_BETA_DECAY_LANES

    @pl.when(t == 0)
    def _():
        state_ref[...] = jnp.zeros_like(state_ref)

    def head_slice(x, i):
        return x[:, i * HEAD_DIM:(i + 1) * HEAD_DIM]

    qs, ks, kks, qks = [], [], [], []
    for a in range(khs):
        q = head_slice(q_ref, a).astype(F32)
        k = head_slice(k_ref, a).astype(F32)
        q = q * lax.rsqrt(jnp.sum(q * q, axis=-1, keepdims=True) + L2_EPS) * (HEAD_DIM ** -0.5)
        k = k * lax.rsqrt(jnp.sum(k * k, axis=-1, keepdims=True) + L2_EPS)
        qs.append(q)
        ks.append(k)
        k_b = k.astype(BF16)
        kks.append(_dot_nt(k_b, k_b))
        qks.append(_dot_nt(q.astype(BF16), k_b))

    row = lax.broadcasted_iota(jnp.int32, (tb, tb), 0)
    col = lax.broadcasted_iota(jnp.int32, (tb, tb), 1)
    same_chunk = (row // CHUNK) == (col // CHUNK)
    tril = same_chunk & (col <= row)
    strict = same_chunk & (col < row)

    bd = pltpu.roll(bd_ref[...], (lanes - hv0) % lanes, 1)

    p_cur, rhs, qk_d, q_g, k_g, g_chunk = [], [], [], [], [], []
    for idx in range(n_v):
        a = idx // rep
        beta_c = bd[:, idx:idx + 1]
        gcum_c = bd[:, n_heads + idx:n_heads + idx + 1]
        gall_c = bd[:, 2 * n_heads + idx:2 * n_heads + idx + 1]
        gcum_r = bdt_ref[pl.ds(n_heads + hv0 + idx, 1), :]
        e = jnp.exp(gcum_c - gcum_r)
        p_cur.append((kks[a] * jnp.where(strict, e, 0.0) * (-beta_c)).astype(BF16))
        qk_d.append((qks[a] * jnp.where(tril, e, 0.0)).astype(BF16))
        exp_g = jnp.exp(gcum_c)
        rhs.append(jnp.concatenate([head_slice(v_ref, idx).astype(F32) * beta_c,
                                    ks[a] * (beta_c * exp_g)], axis=-1))
        q_g.append(qs[a] * exp_g)
        k_g.append((ks[a] * jnp.exp(gall_c - gcum_c)).astype(BF16))
        g_chunk.append(jnp.exp(gall_c))

    n_factors = CHUNK.bit_length() - 1
    for j in range(n_factors):
        for idx in range(n_v):
            p_b = p_cur[idx]
            r_b = rhs[idx].astype(BF16)
            if j < n_factors - 1:
                res = _dot(p_b, jnp.concatenate([p_b, r_b], axis=-1))
                p_cur[idx] = res[:, :tb].astype(BF16)
                rhs[idx] = rhs[idx] + res[:, tb:]
            else:
                rhs[idx] = rhs[idx] + _dot(p_b, r_b)

    uw_b = [rhs[idx].astype(BF16) for idx in range(n_v)]
    quw = [_dot(qk_d[idx], uw_b[idx]) for idx in range(n_v)]
    q_eff = [(q_g[idx] - quw[idx][:, HEAD_DIM:]).astype(BF16) for idx in range(n_v)]
    kuw = [[_dot_tn(k_g[idx][c * CHUNK:(c + 1) * CHUNK], uw_b[idx][c * CHUNK:(c + 1) * CHUNK])
            for c in range(n_chunks)] for idx in range(n_v)]

    states = [state_ref[idx] for idx in range(n_v)]
    outs = [[] for _ in range(n_v)]
    for c in range(n_chunks):
        r0 = c * CHUNK
        rows = slice(r0, r0 + CHUNK)
        for idx in range(n_v):
            lhs = jnp.concatenate([kuw[idx][c][:, HEAD_DIM:].astype(BF16), q_eff[idx][rows]],
                                  axis=0)
            ms = _dot(lhs, states[idx].astype(BF16))
            outs[idx].append(ms[HEAD_DIM:] + quw[idx][rows, :HEAD_DIM])
            states[idx] = (states[idx] * g_chunk[idx][r0:r0 + 1, :] - ms[:HEAD_DIM]
                           + kuw[idx][c][:, :HEAD_DIM])

    nw = nw_ref[...]
    for idx in range(n_v):
        state_ref[idx] = states[idx]
        o = jnp.concatenate(outs[idx], axis=0)
        z = head_slice(z_ref, idx).astype(F32)
        o = o * lax.rsqrt(jnp.mean(o * o, axis=-1, keepdims=True) + RMS_EPS) * nw * _silu(z)
        o_ref[:, idx * HEAD_DIM:(idx + 1) * HEAD_DIM] = o.astype(o_ref.dtype)


def _gated_delta(proj, bd, bd_t, norm_w, batch, seq, n_k_heads, n_v_heads, *, tb=256, khs=2):
    n = proj.shape[0]
    rep = n_v_heads // n_k_heads
    khs = _tile(n_k_heads, khs)
    kw = khs * HEAD_DIM
    vw = khs * rep * HEAD_DIM
    key_dim = n_k_heads * HEAD_DIM
    val_dim = n_v_heads * HEAD_DIM
    assert seq % tb == 0 and tb % CHUNK == 0 and tb % HEAD_DIM == 0
    nt = seq // tb
    k_blk = key_dim // kw
    v_blk = 2 * key_dim // vw
    z_blk = (2 * key_dim + val_dim) // vw
    lanes = N_BETA_DECAY_LANES
    return pl.pallas_call(
        functools.partial(_delta_body, tb=tb, n_heads=n_v_heads, rep=rep, khs=khs),
        out_shape=jax.ShapeDtypeStruct((n, val_dim), BF16),
        grid=(batch, n_k_heads // khs, nt),
        in_specs=[
            pl.BlockSpec((tb, kw), lambda b, h, t: (b * nt + t, h)),
            pl.BlockSpec((tb, kw), lambda b, h, t: (b * nt + t, k_blk + h)),
            pl.BlockSpec((tb, vw), lambda b, h, t: (b * nt + t, v_blk + h)),
            pl.BlockSpec((tb, vw), lambda b, h, t: (b * nt + t, z_blk + h)),
            pl.BlockSpec((tb, lanes), lambda b, h, t: (b * nt + t, 0)),
            pl.BlockSpec((lanes, tb), lambda b, h, t: (0, b * nt + t)),
            pl.BlockSpec((1, HEAD_DIM), lambda b, h, t: (0, 0)),
        ],
        out_specs=pl.BlockSpec((tb, vw), lambda b, h, t: (b * nt + t, h)),
        scratch_shapes=[pltpu.VMEM((khs * rep, HEAD_DIM, HEAD_DIM), F32)],
        compiler_params=_params(("parallel", "parallel", "arbitrary")),
        name="gated_delta",
    )(proj, proj, proj, proj, bd, bd_t, norm_w.reshape(1, HEAD_DIM))


def _rope_tables(seq):
    inv = 1.0 / (ROPE_THETA ** (jnp.arange(0, HEAD_DIM, 2, dtype=F32) / HEAD_DIM))
    ang = jnp.arange(seq, dtype=F32)[:, None] * inv[None, :]
    ang = jnp.concatenate([ang, ang], axis=-1)
    sign = jnp.where(jnp.arange(HEAD_DIM) < HEAD_DIM // 2, -1.0, 1.0).astype(F32)
    return jnp.cos(ang), jnp.sin(ang) * sign[None, :]


def _diff_attention_mixer(h, xn, w_qkv, layer, lq1, lk1, lq2, lk2, subln, w_o, lambda_init,
                          batch, seq):
    d = h.shape[1]
    cos, sin = _rope_tables(seq)
    qkv = _proj_rope(xn, w_qkv, layer, cos, sin, seq, d, 2 * d, HEAD_DIM ** -0.5)
    lam_vecs = jnp.stack([lq1, lk1, lq2, lk2]).astype(F32)
    attn = _diff_attention(qkv, lam_vecs, subln, batch, seq, lambda_init)
    return _out_proj(attn, w_o, layer, h)


def _gated_deltanet_mixer(h, xn, w_in, conv_w, layer, a_log, dt_bias, gdn_norm, w_o, batch, seq):
    n_v_heads = a_log.shape[0]
    val_dim = n_v_heads * HEAD_DIM
    conv_dim = conv_w.shape[2]
    key_dim = (conv_dim - val_dim) // 2
    n_k_heads = key_dim // HEAD_DIM
    main = conv_dim + val_dim
    proj = _proj_conv(xn, w_in, conv_w, layer, seq, main)
    bd, bd_t = _beta_decay(xn, w_in[layer, :, main:main + n_v_heads],
                           w_in[layer, :, main + n_v_heads:], a_log, dt_bias)
    o = _gated_delta(proj, bd, bd_t, gdn_norm, batch, seq, n_k_heads, n_v_heads)
    return _out_proj(o, w_o, layer, h)


def kernel(x, ffn1_norm, ffn1_w_gate, ffn1_w_up, ffn1_w_down, mix_norm, ffn2_norm, ffn2_w_gate, ffn2_w_up, ffn2_w_down, da_w_qkv, da_lambda_q1, da_lambda_k1, da_lambda_q2, da_lambda_k2, da_subln, da_w_o, gdn_w_in, gdn_conv_w, gdn_a_log, gdn_dt_bias, gdn_norm, gdn_w_o, final_norm):
    batch, seq, d = x.shape
    depth = ffn1_norm.shape[0]
    h = x.reshape(batch * seq, d)
    for i in range(depth):
        h, xn = _ffn(h, ffn1_norm[i], ffn1_w_gate, ffn1_w_up, ffn1_w_down, i, "next", mix_norm[i])
        j = i // 2
        if i % 2 == 0:
            lambda_init = 0.8 - 0.6 * math.exp(-0.3 * i)
            h = _diff_attention_mixer(h, xn, da_w_qkv, j, da_lambda_q1[j], da_lambda_k1[j],
                                      da_lambda_q2[j], da_lambda_k2[j], da_subln[j], da_w_o,
                                      lambda_init, batch, seq)
        else:
            h = _gated_deltanet_mixer(h, xn, gdn_w_in, gdn_conv_w, j, gdn_a_log[j],
                                      gdn_dt_bias[j], gdn_norm[j], gdn_w_o, batch, seq)
        if i == depth - 1:
            h = _ffn(h, ffn2_norm[i], ffn2_w_gate, ffn2_w_up, ffn2_w_down, i, "final", final_norm)
        else:
            h = _ffn(h, ffn2_norm[i], ffn2_w_gate, ffn2_w_up, ffn2_w_down, i)
    return h.reshape(batch, seq, d)
```

```python
import functools
import math

import jax
import jax.numpy as jnp
from jax import lax
from jax.experimental import pallas as pl
from jax.experimental.pallas import tpu as pltpu

F32 = jnp.float32
BF16 = jnp.bfloat16

RMS_EPS = 1e-6
L2_EPS = 1e-6
ROPE_THETA = 10000.0
HEAD_DIM = 128
DA_V_DIM = 2 * HEAD_DIM
CHUNK = 64
CONV_WIDTH = 4
CONV_HALO = 16
MXU_COLS = 256
N_BETA_DECAY_LANES = 128
VMEM_LIMIT = 56 * 1024 * 1024
MASK_VALUE = -1e30


def _params(semantics):
    return pltpu.CompilerParams(dimension_semantics=semantics, vmem_limit_bytes=VMEM_LIMIT)


def _tile(total, preferred):
    tile = min(preferred, total)
    while total % tile:
        tile //= 2
    return tile


def _rms_normalize(x, w):
    ms = jnp.mean(x * x, axis=-1, keepdims=True)
    return x * lax.rsqrt(ms + RMS_EPS) * w


def _silu(x):
    return x * jax.nn.sigmoid(x)


def _dot(a, b):
    return jnp.dot(a, b, preferred_element_type=F32)


def _dot_nt(a, b):
    return lax.dot_general(a, b, (((1,), (1,)), ((), ())), preferred_element_type=F32)


def _dot_tn(a, b):
    return lax.dot_general(a, b, (((0,), (0,)), ((), ())), preferred_element_type=F32)


def _dot_exact(a, b):
    return jnp.dot(a, b, preferred_element_type=F32, precision=lax.Precision.HIGHEST)


def _ffn_body(h_ref, nw_ref, wg_ref, wu_ref, wd_ref, ew_ref, o_ref, xn_ref, *, epilogue):
    j = pl.program_id(1)

    @pl.when(j == 0)
    def _():
        xn_ref[...] = _rms_normalize(h_ref[...], nw_ref[...]).astype(BF16)
        o_ref[...] = jnp.zeros_like(o_ref)

    xn = xn_ref[...]
    g = _dot(xn, wg_ref[...].astype(BF16))
    u = _dot(xn, wu_ref[...].astype(BF16))
    a = (_silu(g) * u).astype(BF16)
    o_ref[...] += _dot(a, wd_ref[...].astype(BF16))

    @pl.when(j == pl.num_programs(1) - 1)
    def _():
        y = h_ref[...] + 0.5 * o_ref[...]
        if epilogue == "final":
            o_ref[...] = _rms_normalize(y, ew_ref[...])
        else:
            o_ref[...] = y
            if epilogue == "next":
                xn_ref[...] = _rms_normalize(y, ew_ref[...]).astype(BF16)


def _ffn(h, norm_w, w_gate, w_up, w_down, layer, epilogue=None, epilogue_w=None, *,
         tm=1024, tf=256):
    n, d = h.shape
    f = w_gate.shape[2]
    tm, tf = _tile(n, tm), _tile(f, tf)
    ew = norm_w if epilogue_w is None else epilogue_w
    row_block = pl.BlockSpec((tm, d), lambda i, j: (i, 0))
    out_shape = jax.ShapeDtypeStruct((n, d), F32)
    out_specs = row_block
    scratch_shapes = [pltpu.VMEM((tm, d), BF16)]
    if epilogue == "next":
        out_shape = (out_shape, jax.ShapeDtypeStruct((n, d), BF16))
        out_specs = (row_block, row_block)
        scratch_shapes = []
    return pl.pallas_call(
        functools.partial(_ffn_body, epilogue=epilogue),
        out_shape=out_shape,
        grid=(n // tm, f // tf),
        in_specs=[
            pl.BlockSpec((tm, d), lambda i, j: (i, 0), pipeline_mode=pl.Buffered(1)),
            pl.BlockSpec((1, d), lambda i, j: (0, 0)),
            pl.BlockSpec((None, d, tf), lambda i, j: (layer, 0, j)),
            pl.BlockSpec((None, d, tf), lambda i, j: (layer, 0, j)),
            pl.BlockSpec((None, tf, d), lambda i, j: (layer, j, 0)),
            pl.BlockSpec((1, d), lambda i, j: (0, 0)),
        ],
        out_specs=out_specs,
        scratch_shapes=scratch_shapes,
        compiler_params=_params(("parallel", "arbitrary")),
        name="ffn",
    )(h, norm_w.reshape(1, d), w_gate, w_up, w_down, ew.reshape(1, d))


def _proj_rope_body(x_ref, w_ref, cos_ref, sin_ref, o_ref, wb_ref, *,
                    n_q_tiles, n_rope_tiles, q_scale):
    j = pl.program_id(0)

    @pl.when(pl.program_id(1) == 0)
    def _():
        wb_ref[...] = w_ref[...].astype(BF16)

    tn = o_ref.shape[1]

    @pl.when(j < n_rope_tiles)
    def _():
        cos = cos_ref[...]
        sin = sin_ref[...]
        scale = jnp.where(j < n_q_tiles, q_scale, 1.0).astype(F32)
        for g in range(tn // MXU_COLS):
            y = _dot(x_ref[...], wb_ref[:, g * MXU_COLS:(g + 1) * MXU_COLS])
            for c in range(MXU_COLS // HEAD_DIM):
                x = y[:, c * HEAD_DIM:(c + 1) * HEAD_DIM]
                swapped = pltpu.roll(x, HEAD_DIM // 2, 1)
                lo = g * MXU_COLS + c * HEAD_DIM
                o_ref[:, lo:lo + HEAD_DIM] = (
                    (x * cos + swapped * sin) * scale).astype(o_ref.dtype)

    @pl.when(j >= n_rope_tiles)
    def _():
        o_ref[...] = _dot(x_ref[...], wb_ref[...]).astype(o_ref.dtype)


def _proj_rope(x, w, layer, cos, sin, seq, n_q_cols, n_rope_cols, q_scale, *, tm=1024, tn=1024):
    n, d = x.shape
    n_out = w.shape[2]
    tm, tn = _tile(seq, tm), _tile(n_q_cols, tn)
    assert n % tm == 0 and n_out % tn == 0 and n_rope_cols % tn == 0
    tiles_per_seq = seq // tm
    return pl.pallas_call(
        functools.partial(_proj_rope_body, n_q_tiles=n_q_cols // tn,
                          n_rope_tiles=n_rope_cols // tn, q_scale=q_scale),
        out_shape=jax.ShapeDtypeStruct((n, n_out), BF16),
        grid=(n_out // tn, n // tm),
        in_specs=[
            pl.BlockSpec((tm, d), lambda j, i: (i, 0)),
            pl.BlockSpec((None, d, tn), lambda j, i: (layer, 0, j)),
            pl.BlockSpec((tm, HEAD_DIM), lambda j, i: (i % tiles_per_seq, 0)),
            pl.BlockSpec((tm, HEAD_DIM), lambda j, i: (i % tiles_per_seq, 0)),
        ],
        out_specs=pl.BlockSpec((tm, tn), lambda j, i: (i, j)),
        scratch_shapes=[pltpu.VMEM((d, tn), BF16)],
        compiler_params=_params(("parallel", "arbitrary")),
        name="proj_rope",
    )(x, w, cos, sin)


def _proj_conv_body(x_ref, xh_ref, w_ref, cw_ref, o_ref, wb_ref, *, n_conv_tiles, tiles_per_seq):
    j = pl.program_id(0)
    i = pl.program_id(1)
    tm = x_ref.shape[0]

    @pl.when(i == 0)
    def _():
        wb_ref[...] = w_ref[...].T.astype(BF16)

    keep = jnp.where(i % tiles_per_seq == 0, 0.0, 1.0).astype(BF16)

    @pl.when(j < n_conv_tiles)
    def _():
        lhs = jnp.concatenate([xh_ref[...] * keep, x_ref[...]], axis=0)
        for g in range(o_ref.shape[1] // MXU_COLS):
            cols = slice(g * MXU_COLS, (g + 1) * MXU_COLS)
            y = _dot(lhs, wb_ref[:, cols])
            cw = cw_ref[:, cols]
            acc = None
            for tap in range(CONV_WIDTH):
                off = CONV_HALO - (CONV_WIDTH - 1) + tap
                term = cw[tap:tap + 1, :] * y[off:off + tm, :]
                acc = term if acc is None else acc + term
            o_ref[:, cols] = _silu(acc).astype(o_ref.dtype)

    @pl.when(j >= n_conv_tiles)
    def _():
        o_ref[...] = _dot(x_ref[...], wb_ref[...]).astype(o_ref.dtype)


def _proj_conv(x, w_t, conv_w, layer, seq, n_out, *, tm=1024, tn=1024):
    n, d = x.shape
    n_conv = conv_w.shape[2]
    tm, tn = _tile(seq, tm), _tile(math.gcd(n_conv, n_out), tn)
    assert n % tm == 0 and tm % CONV_HALO == 0
    tiles_per_seq = seq // tm
    n_conv_tiles = n_conv // tn
    halo_blocks = tm // CONV_HALO
    return pl.pallas_call(
        functools.partial(_proj_conv_body, n_conv_tiles=n_conv_tiles,
                          tiles_per_seq=tiles_per_seq),
        out_shape=jax.ShapeDtypeStruct((n, n_out), BF16),
        grid=(n_out // tn, n // tm),
        in_specs=[
            pl.BlockSpec((tm, d), lambda j, i: (i, 0)),
            pl.BlockSpec((CONV_HALO, d), lambda j, i: (jnp.maximum(i * halo_blocks - 1, 0), 0)),
            pl.BlockSpec((None, tn, d), lambda j, i: (layer, j, 0)),
            pl.BlockSpec((None, CONV_WIDTH, tn),
                         lambda j, i: (layer, 0, jnp.minimum(j, n_conv_tiles - 1))),
        ],
        out_specs=pl.BlockSpec((tm, tn), lambda j, i: (i, j)),
        scratch_shapes=[pltpu.VMEM((d, tn), BF16)],
        compiler_params=_params(("parallel", "arbitrary")),
        name="proj_conv",
    )(x, x, w_t, conv_w)


def _out_proj_body(x_ref, w_ref, r_ref, o_ref, wb_ref):
    @pl.when(pl.program_id(1) == 0)
    def _():
        wb_ref[...] = w_ref[...].astype(BF16)

    o_ref[...] = r_ref[...] + _dot(x_ref[...], wb_ref[...])


def _out_proj(x, w, layer, res, *, tm=1024, tn=512):
    n, k = x.shape
    d = w.shape[2]
    tm, tn = _tile(n, tm), _tile(d, tn)
    return pl.pallas_call(
        _out_proj_body,
        out_shape=jax.ShapeDtypeStruct((n, d), F32),
        grid=(d // tn, n // tm),
        in_specs=[
            pl.BlockSpec((tm, k), lambda j, i: (i, 0)),
            pl.BlockSpec((None, k, tn), lambda j, i: (layer, 0, j)),
            pl.BlockSpec((tm, tn), lambda j, i: (i, j)),
        ],
        out_specs=pl.BlockSpec((tm, tn), lambda j, i: (i, j)),
        scratch_shapes=[pltpu.VMEM((k, tn), BF16)],
        compiler_params=_params(("parallel", "arbitrary")),
        name="out_proj",
    )(x, w, res)


def _diff_attn_body(lam_ref, subln_ref, q_ref, k_ref, v_ref, o_ref, *, tq, lambda_init):
    i = pl.program_id(2)
    lv = lam_ref[...]
    lam = (jnp.exp(jnp.sum(lv[0:1] * lv[1:2], axis=-1, keepdims=True))
           - jnp.exp(jnp.sum(lv[2:3] * lv[3:4], axis=-1, keepdims=True)) + lambda_init)

    key_pos = lax.broadcasted_iota(jnp.int32, (tq, tq), 0)
    query_pos = lax.broadcasted_iota(jnp.int32, (tq, tq), 1)
    causal = key_pos <= query_pos
    subs = (0, 1)
    qs = [q_ref[:, s * HEAD_DIM:(s + 1) * HEAD_DIM] for s in subs]

    def step(j, carry, masked):
        start = pl.multiple_of(j * tq, tq)
        vj = v_ref[pl.ds(start, tq), :]
        scores = [_dot_nt(k_ref[pl.ds(start, tq), s * HEAD_DIM:(s + 1) * HEAD_DIM], qs[s])
                  for s in subs]
        new = []
        for s in subs:
            m, l, acc = carry[s]
            sc = jnp.where(causal, scores[s], MASK_VALUE) if masked else scores[s]
            m_new = jnp.maximum(m, jnp.max(sc, axis=0, keepdims=True))
            p = jnp.exp(sc - m_new)
            alpha = jnp.exp(m - m_new)
            l = alpha * l + jnp.sum(p, axis=0, keepdims=True)
            acc = alpha * acc + _dot_tn(vj, p.astype(BF16))
            new.append((m_new, l, acc))
        return tuple(new)

    init = tuple((jnp.full((1, tq), MASK_VALUE, F32), jnp.zeros((1, tq), F32),
                  jnp.zeros((DA_V_DIM, tq), F32)) for _ in subs)
    carry = lax.fori_loop(0, i, functools.partial(step, masked=False), init)
    (_, l0, acc0), (_, l1, acc1) = step(i, carry, True)
    o = acc0 / l0 - lam * (acc1 / l1)
    ms = jnp.mean(o * o, axis=0, keepdims=True)
    y = o * lax.rsqrt(ms + RMS_EPS) * subln_ref[...] * (1.0 - lambda_init)
    o_ref[...] = y.T.astype(o_ref.dtype)


def _diff_attention(qkv, lam_vecs, subln, batch, seq, lambda_init, *, tq=512):
    n, width = qkv.shape
    d = width // 3
    heads = d // DA_V_DIM
    tq = _tile(seq, tq)
    nq = seq // tq
    return pl.pallas_call(
        functools.partial(_diff_attn_body, tq=tq, lambda_init=lambda_init),
        out_shape=jax.ShapeDtypeStruct((n, d), BF16),
        grid=(batch, heads, nq),
        in_specs=[
            pl.BlockSpec((4, HEAD_DIM), lambda b, h, i: (0, 0)),
            pl.BlockSpec((DA_V_DIM, 1), lambda b, h, i: (0, 0)),
            pl.BlockSpec((tq, DA_V_DIM), lambda b, h, i: (b * nq + i, h)),
            pl.BlockSpec((seq, DA_V_DIM), lambda b, h, i: (b, heads + h)),
            pl.BlockSpec((seq, DA_V_DIM), lambda b, h, i: (b, 2 * heads + h)),
        ],
        out_specs=pl.BlockSpec((tq, DA_V_DIM), lambda b, h, i: (b * nq + i, h)),
        compiler_params=_params(("parallel", "parallel", "arbitrary")),
        name="diff_attn",
    )(lam_vecs, subln.reshape(DA_V_DIM, 1), qkv, qkv, qkv)


def _beta_decay_body(x_ref, w_ref, alog_ref, dt_ref, o_ref, ot_ref, *, n_heads):
    tm = x_ref.shape[0]
    y = _dot(x_ref[...], w_ref[...])
    a = y + dt_ref[...]
    softplus = jnp.maximum(a, 0.0) + jnp.log1p(jnp.exp(-jnp.abs(a)))
    g = -jnp.exp(alog_ref[...]) * softplus
    row = lax.broadcasted_iota(jnp.int32, (tm, tm), 0)
    col = lax.broadcasted_iota(jnp.int32, (tm, tm), 1)
    same_chunk = (row // CHUNK) == (col // CHUNK)
    prefix = jnp.where(same_chunk & (col <= row), 1.0, 0.0).astype(F32)
    whole = jnp.where(same_chunk, 1.0, 0.0).astype(F32)
    g_cum = _dot_exact(prefix, g)
    g_all = _dot_exact(whole, g)
    lane = lax.broadcasted_iota(jnp.int32, y.shape, 1)
    packed = jnp.where(lane < n_heads, jax.nn.sigmoid(y),
                       jnp.where(lane < 2 * n_heads, g_cum,
                                 jnp.where(lane < 3 * n_heads, g_all, 0.0)))
    o_ref[...] = packed
    ot_ref[...] = packed.T


def _beta_decay(x, w_b, w_a, a_log, dt_bias, *, tm=256):
    n, d = x.shape
    n_heads = a_log.shape[0]
    lanes = N_BETA_DECAY_LANES
    assert n % tm == 0 and tm % CHUNK == 0 and 3 * n_heads <= lanes
    pad = lanes - 3 * n_heads
    w = jnp.concatenate([w_b, w_a, w_a, jnp.zeros((pad, d), w_a.dtype)], axis=0).T.astype(BF16)
    zeros = jnp.zeros((n_heads,), F32)
    a_log_l = jnp.concatenate([zeros, a_log, a_log, jnp.zeros((pad,), F32)]).reshape(1, lanes)
    dt_l = jnp.concatenate([zeros, dt_bias, dt_bias, jnp.zeros((pad,), F32)]).reshape(1, lanes)
    return pl.pallas_call(
        functools.partial(_beta_decay_body, n_heads=n_heads),
        out_shape=(jax.ShapeDtypeStruct((n, lanes), F32), jax.ShapeDtypeStruct((lanes, n), F32)),
        grid=(n // tm,),
        in_specs=[
            pl.BlockSpec((tm, d), lambda i: (i, 0)),
            pl.BlockSpec((d, lanes), lambda i: (0, 0)),
            pl.BlockSpec((1, lanes), lambda i: (0, 0)),
            pl.BlockSpec((1, lanes), lambda i: (0, 0)),
        ],
        out_specs=(pl.BlockSpec((tm, lanes), lambda i: (i, 0)),
                   pl.BlockSpec((lanes, tm), lambda i: (0, i))),
        compiler_params=_params(("parallel",)),
        name="beta_decay",
    )(x, w, a_log_l, dt_l)


def _delta_body(q_ref, k_ref, v_ref, z_ref, bd_ref, bdt_ref, nw_ref, o_ref, state_ref, *,
                tb, n_heads, rep, khs):
    t = pl.program_id(2)
    hv0 = pl.program_id(1) * (khs * rep)
    n_v = khs * rep
    n_chunks = tb // CHUNK
    lanes = N_BETA_DECAY_LANES

    @pl.when(t == 0)
    def _():
        state_ref[...] = jnp.zeros_like(state_ref)

    def head_slice(x, i):
        return x[:, i * HEAD_DIM:(i + 1) * HEAD_DIM]

    qs, ks, kks, qks = [], [], [], []
    for a in range(khs):
        q = head_slice(q_ref, a).astype(F32)
        k = head_slice(k_ref, a).astype(F32)
        q = q * lax.rsqrt(jnp.sum(q * q, axis=-1, keepdims=True) + L2_EPS) * (HEAD_DIM ** -0.5)
        k = k * lax.rsqrt(jnp.sum(k * k, axis=-1, keepdims=True) + L2_EPS)
        qs.append(q)
        ks.append(k)
        k_b = k.astype(BF16)
        kks.append(_dot_nt(k_b, k_b))
        qks.append(_dot_nt(q.astype(BF16), k_b))

    row = lax.broadcasted_iota(jnp.int32, (tb, tb), 0)
    col = lax.broadcasted_iota(jnp.int32, (tb, tb), 1)
    same_chunk = (row // CHUNK) == (col // CHUNK)
    tril = same_chunk & (col <= row)
    strict = same_chunk & (col < row)

    bd = pltpu.roll(bd_ref[...], (lanes - hv0) % lanes, 1)

    p_cur, rhs, qk_d, q_g, k_g, g_chunk = [], [], [], [], [], []
    for idx in range(n_v):
        a = idx // rep
        beta_c = bd[:, idx:idx + 1]
        gcum_c = bd[:, n_heads + idx:n_heads + idx + 1]
        gall_c = bd[:, 2 * n_heads + idx:2 * n_heads + idx + 1]
        gcum_r = bdt_ref[pl.ds(n_heads + hv0 + idx, 1), :]
        e = jnp.exp(gcum_c - gcum_r)
        p_cur.append((kks[a] * jnp.where(strict, e, 0.0) * (-beta_c)).astype(BF16))
        qk_d.append((qks[a] * jnp.where(tril, e, 0.0)).astype(BF16))
        exp_g = jnp.exp(gcum_c)
        rhs.append(jnp.concatenate([head_slice(v_ref, idx).astype(F32) * beta_c,
                                    ks[a] * (beta_c * exp_g)], axis=-1))
        q_g.append(qs[a] * exp_g)
        k_g.append((ks[a] * jnp.exp(gall_c - gcum_c)).astype(BF16))
        g_chunk.append(jnp.exp(gall_c))

    n_factors = CHUNK.bit_length() - 1
    for j in range(n_factors):
        for idx in range(n_v):
            p_b = p_cur[idx]
            r_b = rhs[idx].astype(BF16)
            if j < n_factors - 1:
                res = _dot(p_b, jnp.concatenate([p_b, r_b], axis=-1))
                p_cur[idx] = res[:, :tb].astype(BF16)
                rhs[idx] = rhs[idx] + res[:, tb:]
            else:
                rhs[idx] = rhs[idx] + _dot(p_b, r_b)

    uw_b = [rhs[idx].astype(BF16) for idx in range(n_v)]
    quw = [_dot(qk_d[idx], uw_b[idx]) for idx in range(n_v)]
    q_eff = [(q_g[idx] - quw[idx][:, HEAD_DIM:]).astype(BF16) for idx in range(n_v)]
    kuw = [[_dot_tn(k_g[idx][c * CHUNK:(c + 1) * CHUNK], uw_b[idx][c * CHUNK:(c + 1) * CHUNK])
            for c in range(n_chunks)] for idx in range(n_v)]

    states = [state_ref[idx] for idx in range(n_v)]
    outs = [[] for _ in range(n_v)]
    for c in range(n_chunks):
        r0 = c * CHUNK
        rows = slice(r0, r0 + CHUNK)
        for idx in range(n_v):
            lhs = jnp.concatenate([kuw[idx][c][:, HEAD_DIM:].astype(BF16), q_eff[idx][rows]],
                                  axis=0)
            ms = _dot(lhs, states[idx].astype(BF16))
            outs[idx].append(ms[HEAD_DIM:] + quw[idx][rows, :HEAD_DIM])
            states[idx] = (states[idx] * g_chunk[idx][r0:r0 + 1, :] - ms[:HEAD_DIM]
                           + kuw[idx][c][:, :HEAD_DIM])

    nw = nw_ref[...]
    for idx in range(n_v):
        state_ref[idx] = states[idx]
        o = jnp.concatenate(outs[idx], axis=0)
        z = head_slice(z_ref, idx).astype(F32)
        o = o * lax.rsqrt(jnp.mean(o * o, axis=-1, keepdims=True) + RMS_EPS) * nw * _silu(z)
        o_ref[:, idx * HEAD_DIM:(idx + 1) * HEAD_DIM] = o.astype(o_ref.dtype)


def _gated_delta(proj, bd, bd_t, norm_w, batch, seq, n_k_heads, n_v_heads, *, tb=256, khs=2):
    n = proj.shape[0]
    rep = n_v_heads // n_k_heads
    khs = _tile(n_k_heads, khs)
    kw = khs * HEAD_DIM
    vw = khs * rep * HEAD_DIM
    key_dim = n_k_heads * HEAD_DIM
    val_dim = n_v_heads * HEAD_DIM
    assert seq % tb == 0 and tb % CHUNK == 0 and tb % HEAD_DIM == 0
    nt = seq // tb
    k_blk = key_dim // kw
    v_blk = 2 * key_dim // vw
    z_blk = (2 * key_dim + val_dim) // vw
    lanes = N_BETA_DECAY_LANES
    return pl.pallas_call(
        functools.partial(_delta_body, tb=tb, n_heads=n_v_heads, rep=rep, khs=khs),
        out_shape=jax.ShapeDtypeStruct((n, val_dim), BF16),
        grid=(batch, n_k_heads // khs, nt),
        in_specs=[
            pl.BlockSpec((tb, kw), lambda b, h, t: (b * nt + t, h)),
            pl.BlockSpec((tb, kw), lambda b, h, t: (b * nt + t, k_blk + h)),
            pl.BlockSpec((tb, vw), lambda b, h, t: (b * nt + t, v_blk + h)),
            pl.BlockSpec((tb, vw), lambda b, h, t: (b * nt + t, z_blk + h)),
            pl.BlockSpec((tb, lanes), lambda b, h, t: (b * nt + t, 0)),
            pl.BlockSpec((lanes, tb), lambda b, h, t: (0, b * nt + t)),
            pl.BlockSpec((1, HEAD_DIM), lambda b, h, t: (0, 0)),
        ],
        out_specs=pl.BlockSpec((tb, vw), lambda b, h, t: (b * nt + t, h)),
        scratch_shapes=[pltpu.VMEM((khs * rep, HEAD_DIM, HEAD_DIM), F32)],
        compiler_params=_params(("parallel", "parallel", "arbitrary")),
        name="gated_delta",
    )(proj, proj, proj, proj, bd, bd_t, norm_w.reshape(1, HEAD_DIM))


def _rope_tables(seq):
    inv = 1.0 / (ROPE_THETA ** (jnp.arange(0, HEAD_DIM, 2, dtype=F32) / HEAD_DIM))
    ang = jnp.arange(seq, dtype=F32)[:, None] * inv[None, :]
    ang = jnp.concatenate([ang, ang], axis=-1)
    sign = jnp.where(jnp.arange(HEAD_DIM) < HEAD_DIM // 2, -1.0, 1.0).astype(F32)
    return jnp.cos(ang), jnp.sin(ang) * sign[None, :]


def _diff_attention_mixer(h, xn, w_qkv, layer, lq1, lk1, lq2, lk2, subln, w_o, lambda_init,
                          batch, seq):
    d = h.shape[1]
    cos, sin = _rope_tables(seq)
    qkv = _proj_rope(xn, w_qkv, layer, cos, sin, seq, d, 2 * d, HEAD_DIM ** -0.5)
    lam_vecs = jnp.stack([lq1, lk1, lq2, lk2]).astype(F32)
    attn = _diff_attention(qkv, lam_vecs, subln, batch, seq, lambda_init)
    return _out_proj(attn, w_o, layer, h)


def _gated_deltanet_mixer(h, xn, w_in, conv_w, layer, a_log, dt_bias, gdn_norm, w_o, batch, seq):
    n_v_heads = a_log.shape[0]
    val_dim = n_v_heads * HEAD_DIM
    conv_dim = conv_w.shape[2]
    key_dim = (conv_dim - val_dim) // 2
    n_k_heads = key_dim // HEAD_DIM
    main = conv_dim + val_dim
    w_in_t = jnp.swapaxes(w_in, 1, 2)
    proj = _proj_conv(xn, w_in_t, conv_w, layer, seq, main)
    bd, bd_t = _beta_decay(xn, w_in_t[layer, main:main + n_v_heads],
                           w_in_t[layer, main + n_v_heads:], a_log, dt_bias)
    o = _gated_delta(proj, bd, bd_t, gdn_norm, batch, seq, n_k_heads, n_v_heads)
    return _out_proj(o, w_o, layer, h)


def kernel(x, ffn1_norm, ffn1_w_gate, ffn1_w_up, ffn1_w_down, mix_norm, ffn2_norm, ffn2_w_gate, ffn2_w_up, ffn2_w_down, da_w_qkv, da_lambda_q1, da_lambda_k1, da_lambda_q2, da_lambda_k2, da_subln, da_w_o, gdn_w_in, gdn_conv_w, gdn_a_log, gdn_dt_bias, gdn_norm, gdn_w_o, final_norm):
    batch, seq, d = x.shape
    depth = ffn1_norm.shape[0]
    h = x.reshape(batch * seq, d)
    for i in range(depth):
        h, xn = _ffn(h, ffn1_norm[i], ffn1_w_gate, ffn1_w_up, ffn1_w_down, i, "next", mix_norm[i])
        j = i // 2
        if i % 2 == 0:
            lambda_init = 0.8 - 0.6 * math.exp(-0.3 * i)
            h = _diff_attention_mixer(h, xn, da_w_qkv, j, da_lambda_q1[j], da_lambda_k1[j],
                                      da_lambda_q2[j], da_lambda_k2[j], da_subln[j], da_w_o,
                                      lambda_init, batch, seq)
        else:
            h = _gated_deltanet_mixer(h, xn, gdn_w_in, gdn_conv_w, j, gdn_a_log[j],
                                      gdn_dt_bias[j], gdn_norm[j], gdn_w_o, batch, seq)
        if i == depth - 1:
            h = _ffn(h, ffn2_norm[i], ffn2_w_gate, ffn2_w_up, ffn2_w_down, i, "final", final_norm)
        else:
            h = _ffn(h, ffn2_norm[i], ffn2_w_gate, ffn2_w_up, ffn2_w_down, i)
    return h.reshape(batch, seq, d)
```

```python
import functools
import math

import jax
import jax.numpy as jnp
from jax import lax
from jax.experimental import pallas as pl
from jax.experimental.pallas import tpu as pltpu

F32 = jnp.float32
BF16 = jnp.bfloat16

RMS_EPS = 1e-6
L2_EPS = 1e-6
ROPE_THETA = 10000.0
HEAD_DIM = 128
DA_V_DIM = 2 * HEAD_DIM
CHUNK = 64
CONV_WIDTH = 4
CONV_HALO = 16
SUBLANES = 8
MXU_COLS = 256
N_BETA_DECAY_LANES = 128
VMEM_LIMIT = 56 * 1024 * 1024
MASK_VALUE = -1e30


def _params(semantics):
    return pltpu.CompilerParams(dimension_semantics=semantics, vmem_limit_bytes=VMEM_LIMIT)


def _tile(total, preferred):
    tile = min(preferred, total)
    while total % tile:
        tile //= 2
    return tile


def _rms_normalize(x, w):
    ms = jnp.mean(x * x, axis=-1, keepdims=True)
    return x * lax.rsqrt(ms + RMS_EPS) * w


def _silu(x):
    return x * jax.nn.sigmoid(x)


def _dot(a, b):
    return jnp.dot(a, b, preferred_element_type=F32)


def _dot_nt(a, b):
    return lax.dot_general(a, b, (((1,), (1,)), ((), ())), preferred_element_type=F32)


def _dot_tn(a, b):
    return lax.dot_general(a, b, (((0,), (0,)), ((), ())), preferred_element_type=F32)


def _dot_exact(a, b):
    return jnp.dot(a, b, preferred_element_type=F32, precision=lax.Precision.HIGHEST)


def _ffn_body(h_ref, nw_ref, wg_ref, wu_ref, wd_ref, ew_ref, o_ref, xn_ref, *, epilogue):
    j = pl.program_id(1)

    @pl.when(j == 0)
    def _():
        xn_ref[...] = _rms_normalize(h_ref[...], nw_ref[...]).astype(BF16)
        o_ref[...] = jnp.zeros_like(o_ref)

    xn = xn_ref[...]
    g = _dot(xn, wg_ref[...].astype(BF16))
    u = _dot(xn, wu_ref[...].astype(BF16))
    a = (_silu(g) * u).astype(BF16)
    o_ref[...] += _dot(a, wd_ref[...].astype(BF16))

    @pl.when(j == pl.num_programs(1) - 1)
    def _():
        y = h_ref[...] + 0.5 * o_ref[...]
        if epilogue == "final":
            o_ref[...] = _rms_normalize(y, ew_ref[...])
        else:
            o_ref[...] = y
            if epilogue == "next":
                xn_ref[...] = _rms_normalize(y, ew_ref[...]).astype(BF16)


def _ffn(h, norm_w, w_gate, w_up, w_down, layer, epilogue=None, epilogue_w=None, *,
         tm=1024, tf=256):
    n, d = h.shape
    f = w_gate.shape[2]
    tm, tf = _tile(n, tm), _tile(f, tf)
    ew = norm_w if epilogue_w is None else epilogue_w
    row_block = pl.BlockSpec((tm, d), lambda i, j: (i, 0))
    out_shape = jax.ShapeDtypeStruct((n, d), F32)
    out_specs = row_block
    scratch_shapes = [pltpu.VMEM((tm, d), BF16)]
    if epilogue == "next":
        out_shape = (out_shape, jax.ShapeDtypeStruct((n, d), BF16))
        out_specs = (row_block, row_block)
        scratch_shapes = []
    return pl.pallas_call(
        functools.partial(_ffn_body, epilogue=epilogue),
        out_shape=out_shape,
        grid=(n // tm, f // tf),
        in_specs=[
            pl.BlockSpec((tm, d), lambda i, j: (i, 0), pipeline_mode=pl.Buffered(1)),
            pl.BlockSpec((1, d), lambda i, j: (0, 0)),
            pl.BlockSpec((None, d, tf), lambda i, j: (layer, 0, j)),
            pl.BlockSpec((None, d, tf), lambda i, j: (layer, 0, j)),
            pl.BlockSpec((None, tf, d), lambda i, j: (layer, j, 0)),
            pl.BlockSpec((1, d), lambda i, j: (0, 0)),
        ],
        out_specs=out_specs,
        scratch_shapes=scratch_shapes,
        compiler_params=_params(("parallel", "arbitrary")),
        name="ffn",
    )(h, norm_w.reshape(1, d), w_gate, w_up, w_down, ew.reshape(1, d))


def _proj_rope_body(x_ref, w_ref, cos_ref, sin_ref, o_ref, wb_ref, *,
                    n_q_tiles, n_rope_tiles, q_scale):
    j = pl.program_id(0)

    @pl.when(pl.program_id(1) == 0)
    def _():
        wb_ref[...] = w_ref[...].astype(BF16)

    tn = o_ref.shape[1]

    @pl.when(j < n_rope_tiles)
    def _():
        cos = cos_ref[...]
        sin = sin_ref[...]
        scale = jnp.where(j < n_q_tiles, q_scale, 1.0).astype(F32)
        for g in range(tn // MXU_COLS):
            y = _dot(x_ref[...], wb_ref[:, g * MXU_COLS:(g + 1) * MXU_COLS])
            for c in range(MXU_COLS // HEAD_DIM):
                x = y[:, c * HEAD_DIM:(c + 1) * HEAD_DIM]
                swapped = pltpu.roll(x, HEAD_DIM // 2, 1)
                lo = g * MXU_COLS + c * HEAD_DIM
                o_ref[:, lo:lo + HEAD_DIM] = (
                    (x * cos + swapped * sin) * scale).astype(o_ref.dtype)

    @pl.when(j >= n_rope_tiles)
    def _():
        o_ref[...] = _dot(x_ref[...], wb_ref[...]).astype(o_ref.dtype)


def _proj_rope(x, w, layer, cos, sin, seq, n_q_cols, n_rope_cols, q_scale, *, tm=1024, tn=1024):
    n, d = x.shape
    n_out = w.shape[2]
    tm, tn = _tile(seq, tm), _tile(n_q_cols, tn)
    assert n % tm == 0 and n_out % tn == 0 and n_rope_cols % tn == 0
    tiles_per_seq = seq // tm
    return pl.pallas_call(
        functools.partial(_proj_rope_body, n_q_tiles=n_q_cols // tn,
                          n_rope_tiles=n_rope_cols // tn, q_scale=q_scale),
        out_shape=jax.ShapeDtypeStruct((n, n_out), BF16),
        grid=(n_out // tn, n // tm),
        in_specs=[
            pl.BlockSpec((tm, d), lambda j, i: (i, 0)),
            pl.BlockSpec((None, d, tn), lambda j, i: (layer, 0, j)),
            pl.BlockSpec((tm, HEAD_DIM), lambda j, i: (i % tiles_per_seq, 0)),
            pl.BlockSpec((tm, HEAD_DIM), lambda j, i: (i % tiles_per_seq, 0)),
        ],
        out_specs=pl.BlockSpec((tm, tn), lambda j, i: (i, j)),
        scratch_shapes=[pltpu.VMEM((d, tn), BF16)],
        compiler_params=_params(("parallel", "arbitrary")),
        name="proj_rope",
    )(x, w, cos, sin)


def _proj_conv_body(x_ref, xh_ref, w_ref, cw_ref, o_ref, wb_ref, *, n_conv_tiles, tiles_per_seq):
    j = pl.program_id(0)
    i = pl.program_id(1)
    tm = x_ref.shape[0]

    @pl.when(i == 0)
    def _():
        wb_ref[...] = w_ref[...].astype(BF16)

    keep =jnp.where(i % tiles_per_seq == 0, 0.0, 1.0).astype(BF16)

    @pl.when(j < n_conv_tiles)
    def _():
        lhs = jnp.concatenate([xh_ref[...] * keep, x_ref[...]], axis=0)
        for g in range(o_ref.shape[1] // MXU_COLS):
            cols = slice(g * MXU_COLS, (g + 1) * MXU_COLS)
            y = _dot(lhs, wb_ref[:, cols])
            cw = cw_ref[:, cols]
            acc = None
            for tap in range(CONV_WIDTH):
                off = CONV_HALO - (CONV_WIDTH - 1) + tap
                term = cw[tap:tap + 1, :] * y[off:off + tm, :]
                acc = term if acc is None else acc + term
            o_ref[:, cols] = _silu(acc).astype(o_ref.dtype)

    @pl.when(j >= n_conv_tiles)
    def _():
        o_ref[...] = _dot(x_ref[...], wb_ref[...]).astype(o_ref.dtype)


def _proj_conv(x, w, conv_w, layer, seq, n_out, *, tm=1024, tn=1024):
    n, d = x.shape
    n_conv = conv_w.shape[2]
    tm, tn = _tile(seq, tm), _tile(math.gcd(n_conv, n_out), tn)
    assert n % tm == 0 and tm % CONV_HALO == 0
    tiles_per_seq = seq // tm
    n_conv_tiles = n_conv // tn
    halo_blocks = tm // CONV_HALO
    return pl.pallas_call(
        functools.partial(_proj_conv_body, n_conv_tiles=n_conv_tiles,
                          tiles_per_seq=tiles_per_seq),
        out_shape=jax.ShapeDtypeStruct((n, n_out), BF16),
        grid=(n_out // tn, n // tm),
        in_specs=[
            pl.BlockSpec((tm, d), lambda j, i: (i, 0)),
            pl.BlockSpec((CONV_HALO, d), lambda j, i: (jnp.maximum(i * halo_blocks - 1, 0), 0)),
            pl.BlockSpec((None, d, tn), lambda j, i: (layer, 0, j)),
            pl.BlockSpec((None, CONV_WIDTH, tn),
                         lambda j, i: (layer, 0, jnp.minimum(j, n_conv_tiles - 1))),
        ],
        out_specs=pl.BlockSpec((tm, tn), lambda j, i: (i, j)),
        scratch_shapes=[pltpu.VMEM((d, tn), BF16)],
        compiler_params=_params(("parallel", "arbitrary")),
        name="proj_conv",
    )(x, x, w, conv_w)


def _out_proj_body(x_ref, w_ref, r_ref, o_ref, wb_ref):
    @pl.when(pl.program_id(1) == 0)
    def _():
        wb_ref[...] = w_ref[...].astype(BF16)

    o_ref[...] = r_ref[...] + _dot(x_ref[...], wb_ref[...])


def _out_proj(x, w, layer, res, *, tm=1024, tn=512):
    n, k = x.shape
    d = w.shape[2]
    tm, tn = _tile(n, tm), _tile(d, tn)
    return pl.pallas_call(
        _out_proj_body,
        out_shape=jax.ShapeDtypeStruct((n, d), F32),
        grid=(d // tn, n // tm),
        in_specs=[
            pl.BlockSpec((tm, k), lambda j, i: (i, 0)),
            pl.BlockSpec((None, k, tn), lambda j, i: (layer, 0, j)),
            pl.BlockSpec((tm, tn), lambda j, i: (i, j)),
        ],
        out_specs=pl.BlockSpec((tm, tn), lambda j, i: (i, j)),
        scratch_shapes=[pltpu.VMEM((k, tn), BF16)],
        compiler_params=_params(("parallel", "arbitrary")),
        name="out_proj",
    )(x, w, res)


def _diff_attn_body(lam_ref, subln_ref, q_ref, k_ref, v_ref, o_ref, *, tq, lambda_init):
    i = pl.program_id(2)
    lv = lam_ref[...]
    lam = (jnp.exp(jnp.sum(lv[0:1] * lv[1:2], axis=-1, keepdims=True))
           - jnp.exp(jnp.sum(lv[2:3] * lv[3:4], axis=-1, keepdims=True)) + lambda_init)

    key_pos = lax.broadcasted_iota(jnp.int32, (tq, tq), 0)
    query_pos = lax.broadcasted_iota(jnp.int32, (tq, tq), 1)
    causal = key_pos <= query_pos
    subs = (0, 1)
    qs = [q_ref[:, s * HEAD_DIM:(s + 1) * HEAD_DIM] for s in subs]

    def step(j, carry, masked):
        start = pl.multiple_of(j * tq, tq)
        vj = v_ref[pl.ds(start, tq), :]
        scores = [_dot_nt(k_ref[pl.ds(start, tq), s * HEAD_DIM:(s + 1) * HEAD_DIM], qs[s])
                  for s in subs]
        new = []
        for s in subs:
            m, l, acc = carry[s]
            sc = jnp.where(causal, scores[s], MASK_VALUE) if masked else scores[s]
            m_new = jnp.maximum(m, jnp.max(sc, axis=0, keepdims=True))
            p = jnp.exp(sc - m_new)
            alpha = jnp.exp(m - m_new)
            l = alpha * l + jnp.sum(p, axis=0, keepdims=True)
            acc = alpha * acc + _dot_tn(vj, p.astype(BF16))
            new.append((m_new, l, acc))
        return tuple(new)

    init = tuple((jnp.full((1, tq), MASK_VALUE, F32), jnp.zeros((1, tq), F32),
                  jnp.zeros((DA_V_DIM, tq), F32)) for _ in subs)
    carry = lax.fori_loop(0, i, functools.partial(step, masked=False), init)
    (_, l0, acc0), (_, l1, acc1) = step(i, carry, True)
    o = acc0 / l0 - lam * (acc1 / l1)
    ms = jnp.mean(o * o, axis=0, keepdims=True)
    y = o * lax.rsqrt(ms + RMS_EPS) * subln_ref[...] * (1.0 - lambda_init)
    o_ref[...] = y.T.astype(o_ref.dtype)


def _diff_attention(qkv, lam_vecs, subln, batch, seq, lambda_init, *, tq=512):
    n, width = qkv.shape
    d = width // 3
    heads = d // DA_V_DIM
    tq = _tile(seq, tq)
    nq = seq // tq
    return pl.pallas_call(
        functools.partial(_diff_attn_body, tq=tq, lambda_init=lambda_init),
        out_shape=jax.ShapeDtypeStruct((n, d), BF16),
        grid=(batch, heads, nq),
        in_specs=[
            pl.BlockSpec((4, HEAD_DIM), lambda b, h, i: (0, 0)),
            pl.BlockSpec((DA_V_DIM, 1), lambda b, h, i: (0, 0)),
            pl.BlockSpec((tq, DA_V_DIM), lambda b, h, i: (b * nq + i, h)),
            pl.BlockSpec((seq, DA_V_DIM), lambda b, h, i: (b, heads + h)),
            pl.BlockSpec((seq, DA_V_DIM), lambda b, h, i: (b, 2 * heads + h)),
        ],
        out_specs=pl.BlockSpec((tq, DA_V_DIM), lambda b, h, i: (b * nq + i, h)),
        compiler_params=_params(("parallel", "parallel", "arbitrary")),
        name="diff_attn",
    )(lam_vecs, subln.reshape(DA_V_DIM, 1), qkv, qkv, qkv)


def _beta_decay_body(x_ref, w_ref, alog_ref, dt_ref, o_ref, ot_ref, *, n_heads):
    tm = x_ref.shape[0]
    y = _dot(x_ref[...], w_ref[...])
    a = y + dt_ref[...]
    softplus = jnp.maximum(a, 0.0) + jnp.log1p(jnp.exp(-jnp.abs(a)))
    g = -jnp.exp(alog_ref[...]) * softplus
    row = lax.broadcasted_iota(jnp.int32, (tm, tm), 0)
    col = lax.broadcasted_iota(jnp.int32, (tm, tm), 1)
    same_chunk = (row // CHUNK) == (col // CHUNK)
    prefix = jnp.where(same_chunk & (col <= row), 1.0, 0.0).astype(F32)
    whole = jnp.where(same_chunk, 1.0, 0.0).astype(F32)
    g_cum = _dot_exact(prefix, g)
    g_all = _dot_exact(whole, g)
    lane = lax.broadcasted_iota(jnp.int32, y.shape, 1)
    packed = jnp.where(lane < n_heads, jax.nn.sigmoid(y),
                       jnp.where(lane < 2 * n_heads, g_cum,
                                 jnp.where(lane < 3 * n_heads, g_all, 0.0)))
    o_ref[...] = packed
    ot_ref[...] = packed.T


def _beta_decay(x, w_b, w_a, a_log, dt_bias, *, tm=256):
    n, d = x.shape
    n_heads = a_log.shape[0]
    lanes = N_BETA_DECAY_LANES
    assert n % tm == 0 and tm % CHUNK == 0 and 3 * n_heads <= lanes
    pad = lanes - 3 * n_heads
    w = jnp.concatenate([w_b, w_a, w_a, jnp.zeros((d, pad), w_a.dtype)], axis=1).astype(BF16)
    zeros = jnp.zeros((n_heads,), F32)
    a_log_l = jnp.concatenate([zeros, a_log, a_log, jnp.zeros((pad,), F32)]).reshape(1, lanes)
    dt_l = jnp.concatenate([zeros, dt_bias, dt_bias, jnp.zeros((pad,), F32)]).reshape(1, lanes)
    return pl.pallas_call(
        functools.partial(_beta_decay_body, n_heads=n_heads),
        out_shape=(jax.ShapeDtypeStruct((n, lanes), F32), jax.ShapeDtypeStruct((lanes, n), F32)),
        grid=(n // tm,),
        in_specs=[
            pl.BlockSpec((tm, d), lambda i: (i, 0)),
            pl.BlockSpec((d, lanes), lambda i: (0, 0)),
            pl.BlockSpec((1, lanes), lambda i: (0, 0)),
            pl.BlockSpec((1, lanes), lambda i: (0, 0)),
        ],
        out_specs=(pl.BlockSpec((tm, lanes), lambda i: (i, 0)),
                   pl.BlockSpec((lanes, tm), lambda i: (0, i))),
        compiler_params=_params(("parallel",)),
        name="beta_decay",
    )(x, w, a_log_l, dt_l)


def _delta_body(q_ref, k_ref, v_ref, z_ref, bd_ref, bdt_ref, nw_ref, o_ref,
                state_ref, qeff_ref, oloc_ref, m_ref, n_ref, g_ref, *,
                tb, n_heads, rep, khs, n_groups):
    s = pl.program_id(1)
    n_pairs = pl.num_programs(1) - 1
    n_v = khs * rep
    n_chunks = tb // CHUNK
    lanes = N_BETA_DECAY_LANES
    prev_group = jnp.maximum(s - 1, 0) % n_groups
    hv0 = (jnp.minimum(s, n_pairs - 1) % n_groups) * n_v

    @pl.when(s == 0)
    def _():
        state_ref[...] = jnp.zeros_like(state_ref)
        qeff_ref[...] = jnp.zeros_like(qeff_ref)
        oloc_ref[...] = jnp.zeros_like(oloc_ref)
        m_ref[...] = jnp.zeros_like(m_ref)
        n_ref[...] = jnp.zeros_like(n_ref)
        g_ref[...] = jnp.zeros_like(g_ref)

    def head_slice(x, i):
        return x[:, i * HEAD_DIM:(i + 1) * HEAD_DIM]

    states = [state_ref[prev_group * n_v + idx] for idx in range(n_v)]
    outs = [[] for _ in range(n_v)]
    for c in range(n_chunks):
        rows = slice(c * CHUNK, (c + 1) * CHUNK)
        for idx in range(n_v):
            lhs = jnp.concatenate([m_ref[idx, c], qeff_ref[idx, rows, :]], axis=0)
            ms = _dot(lhs, states[idx].astype(BF16))
            outs[idx].append(ms[HEAD_DIM:] + oloc_ref[idx, rows, :])
            states[idx] = (states[idx] * g_ref[idx, c][0:1, :] - ms[:HEAD_DIM] + n_ref[idx, c])
    nw = nw_ref[...]
    for idx in range(n_v):
        state_ref[prev_group * n_v + idx] = states[idx]
        o = jnp.concatenate(outs[idx], axis=0)
        z = head_slice(z_ref, idx).astype(F32)
        o = o * lax.rsqrt(jnp.mean(o * o, axis=-1, keepdims=True) + RMS_EPS) * nw * _silu(z)
        o_ref[:, idx * HEAD_DIM:(idx + 1) * HEAD_DIM] = o.astype(o_ref.dtype)

    qs, ks, kks, qks = [], [], [], []
    for a in range(khs):
        q = head_slice(q_ref, a).astype(F32)
        k = head_slice(k_ref, a).astype(F32)
        q = q * lax.rsqrt(jnp.sum(q * q, axis=-1, keepdims=True) + L2_EPS) * (HEAD_DIM ** -0.5)
        k = k * lax.rsqrt(jnp.sum(k * k, axis=-1, keepdims=True) + L2_EPS)
        qs.append(q)
        ks.append(k)
        k_b = k.astype(BF16)
        kks.append(_dot_nt(k_b, k_b))
        qks.append(_dot_nt(q.astype(BF16), k_b))

    row = lax.broadcasted_iota(jnp.int32, (tb, tb), 0)
    col = lax.broadcasted_iota(jnp.int32, (tb, tb), 1)
    same_chunk = (row // CHUNK) == (col // CHUNK)
    tril = same_chunk & (col <= row)
    strict = same_chunk & (col < row)

    bd = pltpu.roll(bd_ref[...], (lanes - hv0) % lanes, 1)

    p_cur, rhs, qk_d, q_g, k_g, g_chunk = [], [], [], [], [], []
    for idx in range(n_v):
        a = idx // rep
        beta_c = bd[:, idx:idx + 1]
        gcum_c = bd[:, n_heads + idx:n_heads + idx + 1]
        gall_c = bd[:, 2 * n_heads + idx:2 * n_heads + idx + 1]
        gcum_r = bdt_ref[pl.ds(n_heads + hv0 + idx, 1), :]
        e = jnp.exp(gcum_c - gcum_r)
        p_cur.append((kks[a] * jnp.where(strict, e, 0.0) * (-beta_c)).astype(BF16))
        qk_d.append((qks[a] * jnp.where(tril, e, 0.0)).astype(BF16))
        exp_g = jnp.exp(gcum_c)
        rhs.append(jnp.concatenate([head_slice(v_ref, idx).astype(F32) * beta_c,
                                    ks[a] * (beta_c * exp_g)], axis=-1))
        q_g.append(qs[a] * exp_g)
        k_g.append((ks[a] * jnp.exp(gall_c - gcum_c)).astype(BF16))
        g_chunk.append(jnp.exp(gall_c))

    n_factors = CHUNK.bit_length() - 1
    for j in range(n_factors):
        for idx in range(n_v):
            p_b = p_cur[idx]
            r_b = rhs[idx].astype(BF16)
            if j < n_factors - 1:
                res = _dot(p_b, jnp.concatenate([p_b, r_b], axis=-1))
                p_cur[idx] = res[:, :tb].astype(BF16)
                rhs[idx] = rhs[idx] + res[:, tb:]
            else:
                rhs[idx] = rhs[idx] + _dot(p_b, r_b)

    for idx in range(n_v):
        uw_b = rhs[idx].astype(BF16)
        quw = _dot(qk_d[idx], uw_b)
        oloc_ref[idx] = quw[:, :HEAD_DIM]
        qeff_ref[idx] = (q_g[idx] - quw[:, HEAD_DIM:]).astype(BF16)
        for c in range(n_chunks):
            rows = slice(c * CHUNK, (c + 1) * CHUNK)
            kuw = _dot_tn(k_g[idx][rows], uw_b[rows])
            n_ref[idx, c] = kuw[:, :HEAD_DIM]
            m_ref[idx, c] = kuw[:, HEAD_DIM:].astype(BF16)
            g_ref[idx, c] = jnp.broadcast_to(g_chunk[idx][c * CHUNK:c * CHUNK + 1, :],
                                             (SUBLANES, HEAD_DIM))


def _gated_delta(proj, bd, bd_t, norm_w, batch, seq, n_k_heads, n_v_heads, *, tb=256, khs=2):
    n = proj.shape[0]
    rep = n_v_heads // n_k_heads
    khs = _tile(n_k_heads, khs)
    n_groups = n_k_heads // khs
    n_v = khs * rep
    kw = khs * HEAD_DIM
    vw = n_v * HEAD_DIM
    key_dim = n_k_heads * HEAD_DIM
    val_dim = n_v_heads * HEAD_DIM
    assert seq % tb == 0 and tb % CHUNK == 0 and tb % HEAD_DIM == 0
    nt = seq // tb
    n_chunks = tb // CHUNK
    n_pairs = nt * n_groups
    k_blk = key_dim // kw
    v_blk = 2 * key_dim // vw
    z_blk = (2 * key_dim + val_dim) // vw
    lanes = N_BETA_DECAY_LANES

    def prep(s):
        pair = jnp.minimum(s, n_pairs - 1)
        return pair // n_groups, pair % n_groups

    def fin(s):
        pair = jnp.maximum(s - 1, 0)
        return pair // n_groups, pair % n_groups

    return pl.pallas_call(
        functools.partial(_delta_body, tb=tb, n_heads=n_v_heads, rep=rep, khs=khs,
                          n_groups=n_groups),
        out_shape=jax.ShapeDtypeStruct((n, val_dim), BF16),
        grid=(batch, n_pairs + 1),
        in_specs=[
            pl.BlockSpec((tb, kw), lambda b, s: (b * nt + prep(s)[0], prep(s)[1])),
            pl.BlockSpec((tb, kw), lambda b, s: (b * nt + prep(s)[0], k_blk + prep(s)[1])),
            pl.BlockSpec((tb, vw), lambda b, s: (b * nt + prep(s)[0], v_blk + prep(s)[1])),
            pl.BlockSpec((tb, vw), lambda b, s: (b * nt + fin(s)[0], z_blk + fin(s)[1])),
            pl.BlockSpec((tb, lanes), lambda b, s: (b * nt + prep(s)[0], 0)),
            pl.BlockSpec((lanes, tb), lambda b, s: (0, b * nt + prep(s)[0])),
            pl.BlockSpec((1, HEAD_DIM), lambda b, s: (0, 0)),
        ],
        out_specs=pl.BlockSpec((tb, vw), lambda b, s: (b * nt + fin(s)[0], fin(s)[1])),
        scratch_shapes=[
            pltpu.VMEM((n_v_heads, HEAD_DIM, HEAD_DIM), F32),
            pltpu.VMEM((n_v, tb, HEAD_DIM), BF16),
            pltpu.VMEM((n_v, tb, HEAD_DIM), F32),
            pltpu.VMEM((n_v, n_chunks, HEAD_DIM, HEAD_DIM), BF16),
            pltpu.VMEM((n_v, n_chunks, HEAD_DIM, HEAD_DIM), F32),
            pltpu.VMEM((n_v, n_chunks, SUBLANES, HEAD_DIM), F32),
        ],
        compiler_params=_params(("parallel", "arbitrary")),
        name="gated_delta",
    )(proj, proj, proj, proj, bd, bd_t, norm_w.reshape(1, HEAD_DIM))


def _rope_tables(seq):
    inv = 1.0 / (ROPE_THETA ** (jnp.arange(0, HEAD_DIM, 2, dtype=F32) / HEAD_DIM))
    ang = jnp.arange(seq, dtype=F32)[:, None] * inv[None, :]
    ang = jnp.concatenate([ang, ang], axis=-1)
    sign = jnp.where(jnp.arange(HEAD_DIM) < HEAD_DIM // 2, -1.0, 1.0).astype(F32)
    return jnp.cos(ang), jnp.sin(ang) * sign[None, :]


def _diff_attention_mixer(h, xn, w_qkv, layer, lq1, lk1, lq2, lk2, subln, w_o, lambda_init,
                          batch, seq):
    d = h.shape[1]
    cos, sin = _rope_tables(seq)
    qkv = _proj_rope(xn, w_qkv, layer, cos, sin, seq, d, 2 * d, HEAD_DIM ** -0.5)
    lam_vecs = jnp.stack([lq1, lk1, lq2, lk2]).astype(F32)
    attn = _diff_attention(qkv, lam_vecs, subln, batch, seq, lambda_init)
    return _out_proj(attn, w_o, layer, h)


def _gated_deltanet_mixer(h, xn, w_in, conv_w, layer, a_log, dt_bias, gdn_norm, w_o, batch, seq):
    n_v_heads = a_log.shape[0]
    val_dim = n_v_heads * HEAD_DIM
    conv_dim = conv_w.shape[2]
    key_dim = (conv_dim - val_dim) // 2
    n_k_heads = key_dim // HEAD_DIM
    main = conv_dim + val_dim
    proj = _proj_conv(xn, w_in.astype(BF16), conv_w, layer, seq, main)
    bd, bd_t = _beta_decay(xn, w_in[layer, :, main:main + n_v_heads],
                           w_in[layer, :, main + n_v_heads:], a_log, dt_bias)
    o = _gated_delta(proj, bd, bd_t, gdn_norm, batch, seq, n_k_heads, n_v_heads)
    return _out_proj(o, w_o, layer, h)


def kernel(x, ffn1_norm, ffn1_w_gate, ffn1_w_up, ffn1_w_down, mix_norm, ffn2_norm, ffn2_w_gate, ffn2_w_up, ffn2_w_down, da_w_qkv, da_lambda_q1, da_lambda_k1, da_lambda_q2, da_lambda_k2, da_subln, da_w_o, gdn_w_in, gdn_conv_w, gdn_a_log, gdn_dt_bias, gdn_norm, gdn_w_o, final_norm):
    batch, seq, d = x.shape
    depth = ffn1_norm.shape[0]
    h = x.reshape(batch * seq, d)
    for i in range(depth):
        h, xn = _ffn(h, ffn1_norm[i], ffn1_w_gate, ffn1_w_up, ffn1_w_down, i, "next", mix_norm[i])
        j = i // 2
        if i % 2 == 0:
            lambda_init = 0.8 - 0.6 * math.exp(-0.3 * i)
            h = _diff_attention_mixer(h, xn, da_w_qkv, j, da_lambda_q1[j], da_lambda_k1[j],
                                      da_lambda_q2[j], da_lambda_k2[j], da_subln[j], da_w_o,
                                      lambda_init, batch, seq)
        else:
            h = _gated_deltanet_mixer(h, xn, gdn_w_in, gdn_conv_w, j, gdn_a_log[j],
                                      gdn_dt_bias[j], gdn_norm[j], gdn_w_o, batch, seq)
        if i == depth - 1:
            h = _ffn(h, ffn2_norm[i], ffn2_w_gate, ffn2_w_up, ffn2_w_down, i, "final", final_norm)
        else:
            h = _ffn(h, ffn2_norm[i], ffn2_w_gate, ffn2_w_up, ffn2_w_down, i)
    return h.reshape(batch, seq, d)
```

```python
import functools
import math

import jax
import jax.numpy as jnp
from jax import lax
from jax.experimental import pallas as pl
from jax.experimental.pallas import tpu as pltpu

F32 = jnp.float32
BF16 = jnp.bfloat16

RMS_EPS = 1e-6
L2_EPS = 1e-6
ROPE_THETA = 10000.0
HEAD_DIM = 128
DA_V_DIM = 2 * HEAD_DIM
CHUNK = 64
CONV_WIDTH = 4
CONV_HALO = 16
SUBLANES = 8
MXU_COLS = 256
N_BETA_DECAY_LANES = 128
VMEM_LIMIT = 56 * 1024 * 1024
MASK_VALUE = -1e30


def _params(semantics):
    return pltpu.CompilerParams(dimension_semantics=semantics, vmem_limit_bytes=VMEM_LIMIT)


def _tile(total, preferred):
    tile = min(preferred, total)
    while total % tile:
        tile //= 2
    return tile


def _rms_normalize(x, w):
    ms = jnp.mean(x * x, axis=-1, keepdims=True)
    return x * lax.rsqrt(ms + RMS_EPS) * w


def _silu(x):
    return x * jax.nn.sigmoid(x)


def _dot(a, b):
    return jnp.dot(a, b, preferred_element_type=F32)


def _dot_nt(a, b):
    return lax.dot_general(a, b, (((1,), (1,)), ((), ())), preferred_element_type=F32)


def _dot_tn(a, b):
    return lax.dot_general(a, b, (((0,), (0,)), ((), ())), preferred_element_type=F32)


def _dot_exact(a, b):
    return jnp.dot(a, b, preferred_element_type=F32, precision=lax.Precision.HIGHEST)


def _ffn_body(h_ref, nw_ref, wg_ref, wu_ref, wd_ref, ew_ref, o_ref, xn_ref, *, epilogue):
    j = pl.program_id(1)

    @pl.when(j == 0)
    def _():
        xn_ref[...] = _rms_normalize(h_ref[...], nw_ref[...]).astype(BF16)
        o_ref[...] = jnp.zeros_like(o_ref)

    xn = xn_ref[...]
    g = _dot(xn, wg_ref[...].astype(BF16))
    u = _dot(xn, wu_ref[...].astype(BF16))
    a = (_silu(g) * u).astype(BF16)
    o_ref[...] += _dot(a, wd_ref[...].astype(BF16))

    @pl.when(j == pl.num_programs(1) - 1)
    def _():
        y = h_ref[...] + 0.5 * o_ref[...]
        if epilogue == "final":
            o_ref[...] = _rms_normalize(y, ew_ref[...])
        else:
            o_ref[...] = y
            if epilogue == "next":
                xn_ref[...] = _rms_normalize(y, ew_ref[...]).astype(BF16)


def _ffn(h, norm_w, w_gate, w_up, w_down, layer, epilogue=None, epilogue_w=None, *,
         tm=1024, tf=256):
    n, d = h.shape
    f = w_gate.shape[2]
    tm, tf = _tile(n, tm), _tile(f, tf)
    ew = norm_w if epilogue_w is None else epilogue_w
    row_block = pl.BlockSpec((tm, d), lambda i, j: (i, 0))
    out_shape = jax.ShapeDtypeStruct((n, d), F32)
    out_specs = row_block
    scratch_shapes = [pltpu.VMEM((tm, d), BF16)]
    if epilogue == "next":
        out_shape = (out_shape, jax.ShapeDtypeStruct((n, d), BF16))
        out_specs = (row_block, row_block)
        scratch_shapes = []
    return pl.pallas_call(
        functools.partial(_ffn_body, epilogue=epilogue),
        out_shape=out_shape,
        grid=(n // tm, f // tf),
        in_specs=[
            pl.BlockSpec((tm, d), lambda i, j: (i, 0), pipeline_mode=pl.Buffered(1)),
            pl.BlockSpec((1, d), lambda i, j: (0, 0)),
            pl.BlockSpec((None, d, tf), lambda i, j: (layer, 0, j)),
            pl.BlockSpec((None, d, tf), lambda i, j: (layer, 0, j)),
            pl.BlockSpec((None, tf, d), lambda i, j: (layer, j, 0)),
            pl.BlockSpec((1, d), lambda i, j: (0, 0)),
        ],
        out_specs=out_specs,
        scratch_shapes=scratch_shapes,
        compiler_params=_params(("parallel", "arbitrary")),
        name="ffn",
    )(h, norm_w.reshape(1, d), w_gate, w_up, w_down, ew.reshape(1, d))


def _proj_rope_body(x_ref, w_ref, cos_ref, sin_ref, o_ref, wb_ref, *,
                    n_q_tiles, n_rope_tiles, q_scale):
    j = pl.program_id(0)

    @pl.when(pl.program_id(1) == 0)
    def _():
        wb_ref[...] = w_ref[...].astype(BF16)

    tn = o_ref.shape[1]

    @pl.when(j < n_rope_tiles)
    def _():
        cos = cos_ref[...]
        sin = sin_ref[...]
        scale = jnp.where(j < n_q_tiles, q_scale, 1.0).astype(F32)
        for g in range(tn // MXU_COLS):
            y = _dot(x_ref[...], wb_ref[:, g * MXU_COLS:(g + 1) * MXU_COLS])
            for c in range(MXU_COLS // HEAD_DIM):
                x = y[:, c * HEAD_DIM:(c + 1) * HEAD_DIM]
                swapped = pltpu.roll(x, HEAD_DIM // 2, 1)
                lo = g * MXU_COLS + c * HEAD_DIM
                o_ref[:, lo:lo + HEAD_DIM] = (
                    (x * cos + swapped * sin) * scale).astype(o_ref.dtype)

    @pl.when(j >= n_rope_tiles)
    def _():
        o_ref[...] = _dot(x_ref[...], wb_ref[...]).astype(o_ref.dtype)


def _proj_rope(x, w, layer, cos, sin, seq, n_q_cols, n_rope_cols, q_scale, *, tm=1024, tn=1024):
    n, d = x.shape
    n_out = w.shape[2]
    tm, tn = _tile(seq, tm), _tile(n_q_cols, tn)
    assert n % tm == 0 and n_out % tn == 0 and n_rope_cols % tn == 0
    tiles_per_seq = seq // tm
    return pl.pallas_call(
        functools.partial(_proj_rope_body, n_q_tiles=n_q_cols // tn,
                          n_rope_tiles=n_rope_cols // tn, q_scale=q_scale),
        out_shape=jax.ShapeDtypeStruct((n, n_out), BF16),
        grid=(n_out // tn, n // tm),
        in_specs=[
            pl.BlockSpec((tm, d), lambda j, i: (i, 0)),
            pl.BlockSpec((None, d, tn), lambda j, i: (layer, 0, j)),
            pl.BlockSpec((tm, HEAD_DIM), lambda j, i: (i % tiles_per_seq, 0)),
            pl.BlockSpec((tm, HEAD_DIM), lambda j, i: (i % tiles_per_seq, 0)),
        ],
        out_specs=pl.BlockSpec((tm, tn), lambda j, i: (i, j)),
        scratch_shapes=[pltpu.VMEM((d, tn), BF16)],
        compiler_params=_params(("parallel", "arbitrary")),
        name="proj_rope",
    )(x, w, cos, sin)


def _proj_conv_body(x_ref, xh_ref, w_ref, cw_ref, o_ref, *, n_conv_tiles, tiles_per_seq):
    j = pl.program_id(0)
    i = pl.program_id(1)
    tm = x_ref.shape[0]
    keep = jnp.where(i % tiles_per_seq == 0, 0.0, 1.0).astype(BF16)

    @pl.when(j < n_conv_tiles)
    def _():
        lhs = jnp.concatenate([xh_ref[...] * keep, x_ref[...]], axis=0)
        for g in range(o_ref.shape[1] // MXU_COLS):
            cols = slice(g * MXU_COLS, (g + 1) * MXU_COLS)
            y = _dot(lhs, w_ref[:, cols])
            cw = cw_ref[:, cols]
            acc = None
            for tap in range(CONV_WIDTH):
                off = CONV_HALO - (CONV_WIDTH - 1) + tap
                term = cw[tap:tap + 1, :] * y[off:off + tm, :]
                acc = term if acc is None else acc + term
            o_ref[:, cols] = _silu(acc).astype(o_ref.dtype)

    @pl.when(j >= n_conv_tiles)
    def _():
        o_ref[...] = _dot(x_ref[...], w_ref[...]).astype(o_ref.dtype)


def _proj_conv(x, w, conv_w, layer, seq, n_out, *, tm=1024, tn=2048):
    assert w.dtype == BF16
    n, d = x.shape
    n_conv = conv_w.shape[2]
    tm, tn = _tile(seq, tm), _tile(math.gcd(n_conv, n_out), tn)
    assert n % tm == 0 and tm % CONV_HALO == 0
    tiles_per_seq = seq // tm
    n_conv_tiles = n_conv // tn
    halo_blocks = tm // CONV_HALO
    return pl.pallas_call(
        functools.partial(_proj_conv_body, n_conv_tiles=n_conv_tiles,
                          tiles_per_seq=tiles_per_seq),
        out_shape=jax.ShapeDtypeStruct((n, n_out), BF16),
        grid=(n_out // tn, n // tm),
        in_specs=[
            pl.BlockSpec((tm, d), lambda j, i: (i, 0)),
            pl.BlockSpec((CONV_HALO, d), lambda j, i: (jnp.maximum(i * halo_blocks - 1, 0), 0)),
            pl.BlockSpec((None, d, tn), lambda j, i: (layer, 0, j)),
            pl.BlockSpec((None, CONV_WIDTH, tn),
                         lambda j, i: (layer, 0, jnp.minimum(j, n_conv_tiles - 1))),
        ],
        out_specs=pl.BlockSpec((tm, tn), lambda j, i: (i, j)),
        compiler_params=_params(("parallel", "parallel")),
        name="proj_conv",
    )(x, x, w, conv_w)


def _out_proj_body(x_ref, w_ref, r_ref, o_ref, wb_ref):
    @pl.when(pl.program_id(1) == 0)
    def _():
        wb_ref[...] = w_ref[...].astype(BF16)

    o_ref[...] = r_ref[...] + _dot(x_ref[...], wb_ref[...])


def _out_proj(x, w, layer, res, *, tm=1024, tn=512):
    n, k = x.shape
    d = w.shape[2]
    tm, tn = _tile(n, tm), _tile(d, tn)
    return pl.pallas_call(
        _out_proj_body,
        out_shape=jax.ShapeDtypeStruct((n, d), F32),
        grid=(d // tn, n // tm),
        in_specs=[
            pl.BlockSpec((tm, k), lambda j, i: (i, 0)),
            pl.BlockSpec((None, k, tn), lambda j, i: (layer, 0, j)),
            pl.BlockSpec((tm, tn), lambda j, i: (i, j)),
        ],
        out_specs=pl.BlockSpec((tm, tn), lambda j, i: (i, j)),
        scratch_shapes=[pltpu.VMEM((k, tn), BF16)],
        compiler_params=_params(("parallel", "arbitrary")),
        name="out_proj",
    )(x, w, res)


def _diff_attn_body(lam_ref, subln_ref, q_ref, k_ref, v_ref, o_ref, *, tq, lambda_init):
    i = pl.program_id(2)
    lv = lam_ref[...]
    lam = (jnp.exp(jnp.sum(lv[0:1] * lv[1:2], axis=-1, keepdims=True))
           - jnp.exp(jnp.sum(lv[2:3] * lv[3:4], axis=-1, keepdims=True)) + lambda_init)

    key_pos = lax.broadcasted_iota(jnp.int32, (tq, tq), 0)
    query_pos = lax.broadcasted_iota(jnp.int32, (tq, tq), 1)
    causal = key_pos <= query_pos
    subs = (0, 1)
    qs = [q_ref[:, s * HEAD_DIM:(s + 1) * HEAD_DIM] for s in subs]

    def step(j, carry, masked):
        start = pl.multiple_of(j * tq, tq)
        vj = v_ref[pl.ds(start, tq), :]
        scores = [_dot_nt(k_ref[pl.ds(start, tq), s * HEAD_DIM:(s + 1) * HEAD_DIM], qs[s])
                  for s in subs]
        new = []
        for s in subs:
            m, l, acc = carry[s]
            sc = jnp.where(causal, scores[s], MASK_VALUE) if masked else scores[s]
            m_new = jnp.maximum(m, jnp.max(sc, axis=0, keepdims=True))
            p = jnp.exp(sc - m_new)
            alpha = jnp.exp(m - m_new)
            l = alpha * l + jnp.sum(p, axis=0, keepdims=True)
            acc = alpha * acc + _dot_tn(vj, p.astype(BF16))
            new.append((m_new, l, acc))
        return tuple(new)

    init = tuple((jnp.full((1, tq), MASK_VALUE, F32), jnp.zeros((1, tq), F32),
                  jnp.zeros((DA_V_DIM, tq), F32)) for _ in subs)
    carry = lax.fori_loop(0, i, functools.partial(step, masked=False), init)
    (_, l0, acc0), (_, l1, acc1) = step(i, carry, True)
    o = acc0 / l0 - lam * (acc1 / l1)
    ms = jnp.mean(o * o, axis=0, keepdims=True)
    y = o * lax.rsqrt(ms + RMS_EPS) * subln_ref[...] * (1.0 - lambda_init)
    o_ref[...] = y.T.astype(o_ref.dtype)


def _diff_attention(qkv, lam_vecs, subln, batch, seq, lambda_init, *, tq=512):
    n, width = qkv.shape
    d = width // 3
    heads = d // DA_V_DIM
    tq = _tile(seq, tq)
    nq = seq // tq
    return pl.pallas_call(
        functools.partial(_diff_attn_body, tq=tq, lambda_init=lambda_init),
        out_shape=jax.ShapeDtypeStruct((n, d), BF16),
        grid=(batch, heads, nq),
        in_specs=[
            pl.BlockSpec((4, HEAD_DIM), lambda b, h, i: (0, 0)),
            pl.BlockSpec((DA_V_DIM, 1), lambda b, h, i: (0, 0)),
            pl.BlockSpec((tq, DA_V_DIM), lambda b, h, i: (b * nq + i, h)),
            pl.BlockSpec((seq, DA_V_DIM), lambda b, h, i: (b, heads + h)),
            pl.BlockSpec((seq, DA_V_DIM), lambda b, h, i: (b, 2 * heads + h)),
        ],
        out_specs=pl.BlockSpec((tq, DA_V_DIM), lambda b, h, i: (b * nq + i, h)),
        compiler_params=_params(("parallel", "parallel", "arbitrary")),
        name="diff_attn",
    )(lam_vecs, subln.reshape(DA_V_DIM, 1), qkv, qkv, qkv)


def _beta_decay_body(x_ref, w_ref, alog_ref, dt_ref, o_ref, ot_ref, *, n_heads):
    tm = x_ref.shape[0]
    y = _dot(x_ref[...], w_ref[...])
    a = y + dt_ref[...]
    softplus = jnp.maximum(a, 0.0) + jnp.log1p(jnp.exp(-jnp.abs(a)))
    g = -jnp.exp(alog_ref[...]) * softplus
    row = lax.broadcasted_iota(jnp.int32, (tm, tm), 0)
    col = lax.broadcasted_iota(jnp.int32, (tm, tm), 1)
    same_chunk = (row // CHUNK) == (col // CHUNK)
    prefix = jnp.where(same_chunk & (col <= row), 1.0, 0.0).astype(F32)
    whole = jnp.where(same_chunk, 1.0, 0.0).astype(F32)
    g_cum = _dot_exact(prefix, g)
    g_all = _dot_exact(whole, g)
    lane = lax.broadcasted_iota(jnp.int32, y.shape, 1)
    packed = jnp.where(lane < n_heads, jax.nn.sigmoid(y),
                       jnp.where(lane < 2 * n_heads, g_cum,
                                 jnp.where(lane < 3 * n_heads, g_all, 0.0)))
    o_ref[...] = packed
    ot_ref[...] = packed.T


def _beta_decay(x, w_b, w_a, a_log, dt_bias, *, tm=256):
    n, d = x.shape
    n_heads = a_log.shape[0]
    lanes = N_BETA_DECAY_LANES
    assert n % tm == 0 and tm % CHUNK == 0 and 3 * n_heads <= lanes
    pad = lanes - 3 * n_heads
    w = jnp.concatenate([w_b, w_a, w_a, jnp.zeros((d, pad), w_a.dtype)], axis=1).astype(BF16)
    zeros = jnp.zeros((n_heads,), F32)
    a_log_l = jnp.concatenate([zeros, a_log, a_log, jnp.zeros((pad,), F32)]).reshape(1, lanes)
    dt_l = jnp.concatenate([zeros, dt_bias, dt_bias, jnp.zeros((pad,), F32)]).reshape(1, lanes)
    return pl.pallas_call(
        functools.partial(_beta_decay_body, n_heads=n_heads),
        out_shape=(jax.ShapeDtypeStruct((n, lanes), F32), jax.ShapeDtypeStruct((lanes, n), F32)),
        grid=(n // tm,),
        in_specs=[
            pl.BlockSpec((tm, d), lambda i: (i, 0)),
            pl.BlockSpec((d, lanes), lambda i: (0, 0)),
            pl.BlockSpec((1, lanes), lambda i: (0, 0)),
            pl.BlockSpec((1, lanes), lambda i: (0, 0)),
        ],
        out_specs=(pl.BlockSpec((tm, lanes), lambda i: (i, 0)),
                   pl.BlockSpec((lanes, tm), lambda i: (0, i))),
        compiler_params=_params(("parallel",)),
        name="beta_decay",
    )(x, w, a_log_l, dt_l)


def _delta_body(q_ref, k_ref, v_ref, z_ref, bd_ref, bdt_ref, nw_ref, o_ref,
                state_ref, qeff_ref, oloc_ref, m_ref, n_ref, g_ref, *,
                tb, n_heads, rep, khs, n_groups):
    s = pl.program_id(1)
    n_pairs = pl.num_programs(1) - 1
    n_v = khs * rep
    n_chunks = tb // CHUNK
    lanes = N_BETA_DECAY_LANES
    prev_group = jnp.maximum(s - 1, 0) % n_groups
    hv0 = (jnp.minimum(s, n_pairs - 1) % n_groups) * n_v

    @pl.when(s == 0)
    def _():
        state_ref[...] = jnp.zeros_like(state_ref)
        qeff_ref[...] = jnp.zeros_like(qeff_ref)
        oloc_ref[...] = jnp.zeros_like(oloc_ref)
        m_ref[...] = jnp.zeros_like(m_ref)
        n_ref[...] = jnp.zeros_like(n_ref)
        g_ref[...] = jnp.zeros_like(g_ref)

    def head_slice(x, i):
        return x[:, i * HEAD_DIM:(i + 1) * HEAD_DIM]

    states = [state_ref[prev_group * n_v + idx] for idx in range(n_v)]
    outs = [[] for _ in range(n_v)]
    for c in range(n_chunks):
        rows = slice(c * CHUNK, (c + 1) * CHUNK)
        for idx in range(n_v):
            lhs = jnp.concatenate([m_ref[idx, c], qeff_ref[idx, rows, :]], axis=0)
            ms = _dot(lhs, states[idx].astype(BF16))
            outs[idx].append(ms[HEAD_DIM:] + oloc_ref[idx, rows, :])
            states[idx] = (states[idx] * g_ref[idx, c][0:1, :] - ms[:HEAD_DIM] + n_ref[idx, c])
    nw = nw_ref[...]
    for idx in range(n_v):
        state_ref[prev_group * n_v + idx] = states[idx]
        o = jnp.concatenate(outs[idx], axis=0)
        z = head_slice(z_ref, idx).astype(F32)
        o = o * lax.rsqrt(jnp.mean(o * o, axis=-1, keepdims=True) + RMS_EPS) * nw * _silu(z)
        o_ref[:, idx * HEAD_DIM:(idx + 1) * HEAD_DIM] = o.astype(o_ref.dtype)

    qs, ks, kks, qks = [], [], [], []
    for a in range(khs):
        q = head_slice(q_ref, a).astype(F32)
        k = head_slice(k_ref, a).astype(F32)
        q = q * lax.rsqrt(jnp.sum(q * q, axis=-1, keepdims=True) + L2_EPS) * (HEAD_DIM ** -0.5)
        k = k * lax.rsqrt(jnp.sum(k * k, axis=-1, keepdims=True) + L2_EPS)
        qs.append(q)
        ks.append(k)
        k_b = k.astype(BF16)
        kks.append(_dot_nt(k_b, k_b))
        qks.append(_dot_nt(q.astype(BF16), k_b))

    row = lax.broadcasted_iota(jnp.int32, (tb, tb), 0)
    col = lax.broadcasted_iota(jnp.int32, (tb, tb), 1)
    same_chunk = (row // CHUNK) == (col // CHUNK)
    tril = same_chunk & (col <= row)
    strict = same_chunk & (col < row)

    bd = pltpu.roll(bd_ref[...], (lanes - hv0) % lanes, 1)

    p_cur, rhs, qk_d, q_g, k_g, g_chunk = [], [], [], [], [], []
    for idx in range(n_v):
        a = idx // rep
        beta_c = bd[:, idx:idx + 1]
        gcum_c = bd[:, n_heads + idx:n_heads + idx + 1]
        gall_c = bd[:, 2 * n_heads + idx:2 * n_heads + idx + 1]
        gcum_r = bdt_ref[pl.ds(n_heads + hv0 + idx, 1), :]
        e = jnp.exp(gcum_c - gcum_r)
        p_cur.append((kks[a] * jnp.where(strict, e, 0.0) * (-beta_c)).astype(BF16))
        qk_d.append((qks[a] * jnp.where(tril, e, 0.0)).astype(BF16))
        exp_g = jnp.exp(gcum_c)
        rhs.append(jnp.concatenate([head_slice(v_ref, idx).astype(F32) * beta_c,
                                    ks[a] * (beta_c * exp_g)], axis=-1))
        q_g.append(qs[a] * exp_g)
        k_g.append((ks[a] * jnp.exp(gall_c - gcum_c)).astype(BF16))
        g_chunk.append(jnp.exp(gall_c))

    n_factors = CHUNK.bit_length() - 1
    for j in range(n_factors):
        for idx in range(n_v):
            p_b = p_cur[idx]
            r_b = rhs[idx].astype(BF16)
            if j < n_factors - 1:
                res = _dot(p_b, jnp.concatenate([p_b, r_b], axis=-1))
                p_cur[idx] = res[:, :tb].astype(BF16)
                rhs[idx] = rhs[idx] + res[:, tb:]
            else:
                rhs[idx] = rhs[idx] + _dot(p_b, r_b)

    for idx in range(n_v):
        uw_b = rhs[idx].astype(BF16)
        quw = _dot(qk_d[idx], uw_b)
        oloc_ref[idx] = quw[:, :HEAD_DIM]
        qeff_ref[idx] = (q_g[idx] - quw[:, HEAD_DIM:]).astype(BF16)
        for c in range(n_chunks):
            rows = slice(c * CHUNK, (c + 1) * CHUNK)
            kuw = _dot_tn(k_g[idx][rows], uw_b[rows])
            n_ref[idx, c] = kuw[:, :HEAD_DIM]
            m_ref[idx, c] = kuw[:, HEAD_DIM:].astype(BF16)
            g_ref[idx, c] = jnp.broadcast_to(g_chunk[idx][c * CHUNK:c * CHUNK + 1, :],
                                             (SUBLANES, HEAD_DIM))


def _gated_delta(proj, bd, bd_t, norm_w, batch, seq, n_k_heads, n_v_heads, *, tb=256, khs=4):
    n = proj.shape[0]
    rep = n_v_heads // n_k_heads
    khs = _tile(n_k_heads, khs)
    n_groups = n_k_heads // khs
    n_v = khs * rep
    kw = khs * HEAD_DIM
    vw = n_v * HEAD_DIM
    key_dim = n_k_heads * HEAD_DIM
    val_dim = n_v_heads * HEAD_DIM
    assert seq % tb == 0 and tb % CHUNK == 0 and tb % HEAD_DIM == 0
    nt = seq // tb
    n_chunks = tb // CHUNK
    n_pairs = nt * n_groups
    k_blk = key_dim // kw
    v_blk = 2 * key_dim // vw
    z_blk = (2 * key_dim + val_dim) // vw
    lanes = N_BETA_DECAY_LANES

    def prep(s):
        pair = jnp.minimum(s, n_pairs - 1)
        return pair // n_groups, pair % n_groups

    def fin(s):
        pair = jnp.maximum(s - 1, 0)
        return pair // n_groups, pair % n_groups

    return pl.pallas_call(
        functools.partial(_delta_body, tb=tb, n_heads=n_v_heads, rep=rep, khs=khs,
                          n_groups=n_groups),
        out_shape=jax.ShapeDtypeStruct((n, val_dim), BF16),
        grid=(batch, n_pairs + 1),
        in_specs=[
            pl.BlockSpec((tb, kw), lambda b, s: (b * nt + prep(s)[0], prep(s)[1])),
            pl.BlockSpec((tb, kw), lambda b, s: (b * nt + prep(s)[0], k_blk + prep(s)[1])),
            pl.BlockSpec((tb, vw), lambda b, s: (b * nt + prep(s)[0], v_blk + prep(s)[1])),
            pl.BlockSpec((tb, vw), lambda b, s: (b * nt + fin(s)[0], z_blk + fin(s)[1])),
            pl.BlockSpec((tb, lanes), lambda b, s: (b * nt + prep(s)[0], 0)),
            pl.BlockSpec((lanes, tb), lambda b, s: (0, b * nt + prep(s)[0])),
            pl.BlockSpec((1, HEAD_DIM), lambda b, s: (0, 0)),
        ],
        out_specs=pl.BlockSpec((tb, vw), lambda b, s: (b * nt + fin(s)[0], fin(s)[1])),
        scratch_shapes=[
            pltpu.VMEM((n_v_heads, HEAD_DIM, HEAD_DIM), F32),
            pltpu.VMEM((n_v, tb, HEAD_DIM), BF16),
            pltpu.VMEM((n_v, tb, HEAD_DIM), F32),
            pltpu.VMEM((n_v, n_chunks, HEAD_DIM, HEAD_DIM), BF16),
            pltpu.VMEM((n_v, n_chunks, HEAD_DIM, HEAD_DIM), F32),
            pltpu.VMEM((n_v, n_chunks, SUBLANES, HEAD_DIM), F32),
        ],
        compiler_params=_params(("parallel", "arbitrary")),
        name="gated_delta",
    )(proj, proj, proj, proj, bd, bd_t, norm_w.reshape(1, HEAD_DIM))


def _rope_tables(seq):
    inv = 1.0 / (ROPE_THETA ** (jnp.arange(0, HEAD_DIM, 2, dtype=F32) / HEAD_DIM))
    ang = jnp.arange(seq, dtype=F32)[:, None] * inv[None, :]
    ang = jnp.concatenate([ang, ang], axis=-1)
    sign = jnp.where(jnp.arange(HEAD_DIM) < HEAD_DIM // 2, -1.0, 1.0).astype(F32)
    return jnp.cos(ang), jnp.sin(ang) * sign[None, :]


def _diff_attention_mixer(h, xn, w_qkv, layer, lq1, lk1, lq2, lk2, subln, w_o, lambda_init,
                          batch, seq):
    d = h.shape[1]
    cos, sin = _rope_tables(seq)
    qkv = _proj_rope(xn, w_qkv, layer, cos, sin, seq, d, 2 * d, HEAD_DIM ** -0.5)
    lam_vecs = jnp.stack([lq1, lk1, lq2, lk2]).astype(F32)
    attn = _diff_attention(qkv, lam_vecs, subln, batch, seq, lambda_init)
    return _out_proj(attn, w_o, layer, h)


def _gated_deltanet_mixer(h, xn, w_in, conv_w, layer, a_log, dt_bias, gdn_norm, w_o, batch, seq):
    n_v_heads = a_log.shape[0]
    val_dim = n_v_heads * HEAD_DIM
    conv_dim = conv_w.shape[2]
    key_dim = (conv_dim - val_dim) // 2
    n_k_heads = key_dim // HEAD_DIM
    main = conv_dim + val_dim
    proj = _proj_conv(xn, w_in.astype(BF16), conv_w, layer, seq, main)
    bd, bd_t = _beta_decay(xn, w_in[layer, :, main:main + n_v_heads],
                           w_in[layer, :, main + n_v_heads:], a_log, dt_bias)
    o = _gated_delta(proj, bd, bd_t, gdn_norm, batch, seq, n_k_heads, n_v_heads)
    return _out_proj(o, w_o, layer, h)


def kernel(x, ffn1_norm, ffn1_w_gate, ffn1_w_up, ffn1_w_down, mix_norm, ffn2_norm, ffn2_w_gate, ffn2_w_up, ffn2_w_down, da_w_qkv, da_lambda_q1, da_lambda_k1, da_lambda_q2, da_lambda_k2, da_subln, da_w_o, gdn_w_in, gdn_conv_w, gdn_a_log, gdn_dt_bias, gdn_norm, gdn_w_o, final_norm):
    batch, seq, d = x.shape
    depth = ffn1_norm.shape[0]
    h = x.reshape(batch * seq, d)
    for i in range(depth):
        h, xn = _ffn(h, ffn1_norm[i], ffn1_w_gate, ffn1_w_up, ffn1_w_down, i, "next", mix_norm[i])
        j = i // 2
        if i % 2 == 0:
            lambda_init = 0.8 - 0.6 * math.exp(-0.3 * i)
            h = _diff_attention_mixer(h, xn, da_w_qkv, j, da_lambda_q1[j], da_lambda_k1[j],
                                      da_lambda_q2[j], da_lambda_k2[j], da_subln[j], da_w_o,
                                      lambda_init, batch, seq)
        else:
            h = _gated_deltanet_mixer(h, xn, gdn_w_in, gdn_conv_w, j, gdn_a_log[j],
                                      gdn_dt_bias[j], gdn_norm[j], gdn_w_o, batch, seq)
        if i == depth - 1:
            h = _ffn(h, ffn2_norm[i], ffn2_w_gate, ffn2_w_up, ffn2_w_down, i, "final", final_norm)
        else:
            h = _ffn(h, ffn2_norm[i], ffn2_w_gate, ffn2_w_up, ffn2_w_down, i)
    return h.reshape(batch, seq, d)
```

```python
import functools
import math

import jax
import jax.numpy as jnp
from jax import lax
from jax.experimental import pallas as pl
from jax.experimental.pallas import tpu as pltpu

F32 = jnp.float32
BF16 = jnp.bfloat16

RMS_EPS = 1e-6
L2_EPS = 1e-6
ROPE_THETA = 10000.0
HEAD_DIM = 128
DA_V_DIM = 2 * HEAD_DIM
CHUNK = 64
CONV_WIDTH = 4
CONV_HALO = 16
SUBLANES = 8
MXU_COLS = 256
N_BETA_DECAY_LANES = 128
VMEM_LIMIT = 56 * 1024 * 1024
MASK_VALUE = -1e30


def _params(semantics):
    return pltpu.CompilerParams(dimension_semantics=semantics, vmem_limit_bytes=VMEM_LIMIT)


def _tile(total, preferred):
    tile = min(preferred, total)
    while total % tile:
        tile //= 2
    return tile


def _rms_normalize(x, w):
    ms = jnp.mean(x * x, axis=-1, keepdims=True)
    return x * lax.rsqrt(ms + RMS_EPS) * w


def _silu(x):
    return x * jax.nn.sigmoid(x)


def _dot(a, b):
    return jnp.dot(a, b, preferred_element_type=F32)


def _dot_nt(a, b):
    return lax.dot_general(a, b, (((1,), (1,)), ((), ())), preferred_element_type=F32)


def _dot_tn(a, b):
    return lax.dot_general(a, b, (((0,), (0,)), ((), ())), preferred_element_type=F32)


def _dot_exact(a, b):
    return jnp.dot(a, b, preferred_element_type=F32, precision=lax.Precision.HIGHEST)


def _ffn_body(h_ref, nw_ref, wg_ref, wu_ref, wd_ref, ew_ref, o_ref, xn_ref, *, epilogue):
    j = pl.program_id(1)

    @pl.when(j == 0)
    def _():
        xn_ref[...] = _rms_normalize(h_ref[...], nw_ref[...]).astype(BF16)
        o_ref[...] = jnp.zeros_like(o_ref)

    xn = xn_ref[...]
    g = _dot(xn, wg_ref[...].astype(BF16))
    u = _dot(xn, wu_ref[...].astype(BF16))
    a = (_silu(g) * u).astype(BF16)
    o_ref[...] += _dot(a, wd_ref[...].astype(BF16))

    @pl.when(j == pl.num_programs(1) - 1)
    def _():
        y = h_ref[...] + 0.5 * o_ref[...]
        if epilogue == "final":
            o_ref[...] = _rms_normalize(y, ew_ref[...])
        else:
            o_ref[...] = y
            if epilogue == "next":
                xn_ref[...] = _rms_normalize(y, ew_ref[...]).astype(BF16)


def _ffn(h, norm_w, w_gate, w_up, w_down, layer, epilogue=None, epilogue_w=None, *,
         tm=1024, tf=256):
    n, d = h.shape
    f = w_gate.shape[2]
    tm, tf = _tile(n, tm), _tile(f, tf)
    ew = norm_w if epilogue_w is None else epilogue_w
    row_block = pl.BlockSpec((tm, d), lambda i, j: (i, 0))
    out_shape = jax.ShapeDtypeStruct((n, d), F32)
    out_specs = row_block
    scratch_shapes = [pltpu.VMEM((tm, d), BF16)]
    if epilogue == "next":
        out_shape = (out_shape, jax.ShapeDtypeStruct((n, d), BF16))
        out_specs = (row_block, row_block)
        scratch_shapes = []
    return pl.pallas_call(
        functools.partial(_ffn_body, epilogue=epilogue),
        out_shape=out_shape,
        grid=(n // tm, f // tf),
        in_specs=[
            pl.BlockSpec((tm, d), lambda i, j: (i, 0), pipeline_mode=pl.Buffered(1)),
            pl.BlockSpec((1, d), lambda i, j: (0, 0)),
            pl.BlockSpec((None, d, tf), lambda i, j: (layer, 0, j)),
            pl.BlockSpec((None, d, tf), lambda i, j: (layer, 0, j)),
            pl.BlockSpec((None, tf, d), lambda i, j: (layer, j, 0)),
            pl.BlockSpec((1, d), lambda i, j: (0, 0)),
        ],
        out_specs=out_specs,
        scratch_shapes=scratch_shapes,
        compiler_params=_params(("parallel", "arbitrary")),
        name="ffn",
    )(h, norm_w.reshape(1, d), w_gate, w_up, w_down, ew.reshape(1, d))


def _proj_rope_body(x_ref, w_ref, cos_ref, sin_ref, o_ref, wb_ref, *,
                    n_q_tiles, n_rope_tiles, q_scale):
    j = pl.program_id(0)

    @pl.when(pl.program_id(1) == 0)
    def _():
        wb_ref[...] = w_ref[...].astype(BF16)

    tn = o_ref.shape[1]

    @pl.when(j < n_rope_tiles)
    def _():
        cos = cos_ref[...]
        sin = sin_ref[...]
        scale = jnp.where(j < n_q_tiles, q_scale, 1.0).astype(F32)
        for g in range(tn // MXU_COLS):
            y = _dot(x_ref[...], wb_ref[:, g * MXU_COLS:(g + 1) * MXU_COLS])
            for c in range(MXU_COLS // HEAD_DIM):
                x = y[:, c * HEAD_DIM:(c + 1) * HEAD_DIM]
                swapped = pltpu.roll(x, HEAD_DIM // 2, 1)
                lo = g * MXU_COLS + c * HEAD_DIM
                o_ref[:, lo:lo + HEAD_DIM] = (
                    (x * cos + swapped * sin) * scale).astype(o_ref.dtype)

    @pl.when(j >= n_rope_tiles)
    def _():
        o_ref[...] = _dot(x_ref[...], wb_ref[...]).astype(o_ref.dtype)


def _proj_rope(x, w, layer, cos, sin, seq, n_q_cols, n_rope_cols, q_scale, *, tm=1024, tn=1024):
    n, d = x.shape
    n_out = w.shape[2]
    tm, tn = _tile(seq, tm), _tile(n_q_cols, tn)
    assert n % tm == 0 and n_out % tn == 0 and n_rope_cols % tn == 0
    tiles_per_seq = seq // tm
    return pl.pallas_call(
        functools.partial(_proj_rope_body, n_q_tiles=n_q_cols // tn,
                          n_rope_tiles=n_rope_cols // tn, q_scale=q_scale),
        out_shape=jax.ShapeDtypeStruct((n, n_out), BF16),
        grid=(n_out // tn, n // tm),
        in_specs=[
            pl.BlockSpec((tm, d), lambda j, i: (i, 0)),
            pl.BlockSpec((None, d, tn), lambda j, i: (layer, 0, j)),
            pl.BlockSpec((tm, HEAD_DIM), lambda j, i: (i % tiles_per_seq, 0)),
            pl.BlockSpec((tm, HEAD_DIM), lambda j, i: (i % tiles_per_seq, 0)),
        ],
        out_specs=pl.BlockSpec((tm, tn), lambda j, i: (i, j)),
        scratch_shapes=[pltpu.VMEM((d, tn), BF16)],
        compiler_params=_params(("parallel", "arbitrary")),
        name="proj_rope",
    )(x, w, cos, sin)


def _proj_conv_body(x_ref, xh_ref, w_ref, cw_ref, o_ref, *, n_conv_tiles, tiles_per_seq):
    j = pl.program_id(0)
    i = pl.program_id(1)
    tm = x_ref.shape[0]
    keep = jnp.where(i % tiles_per_seq == 0, 0.0, 1.0).astype(BF16)

    @pl.when(j < n_conv_tiles)
    def _():
        lhs = jnp.concatenate([xh_ref[...] * keep, x_ref[...]], axis=0)
        for g in range(o_ref.shape[1] // MXU_COLS):
            cols = slice(g * MXU_COLS, (g + 1) * MXU_COLS)
            y = _dot(lhs, w_ref[:, cols])
            cw = cw_ref[:, cols]
            acc = None
            for tap in range(CONV_WIDTH):
                off = CONV_HALO - (CONV_WIDTH - 1) + tap
                term = cw[tap:tap + 1, :] * y[off:off + tm, :]
                acc = term if acc is None else acc + term
            o_ref[:, cols] = _silu(acc).astype(o_ref.dtype)

    @pl.when(j >= n_conv_tiles)
    def _():
        o_ref[...] = _dot(x_ref[...], w_ref[...]).astype(o_ref.dtype)


def _proj_conv(x, w, conv_w, layer, seq, n_out, *, tm=1024, tn=2048):
    assert w.dtype == BF16
    n, d = x.shape
    n_conv = conv_w.shape[2]
    tm, tn = _tile(seq, tm), _tile(math.gcd(n_conv, n_out), tn)
    assert n % tm == 0 and tm % CONV_HALO == 0
    tiles_per_seq = seq // tm
    n_conv_tiles = n_conv // tn
    halo_blocks = tm // CONV_HALO
    return pl.pallas_call(
        functools.partial(_proj_conv_body, n_conv_tiles=n_conv_tiles,
                          tiles_per_seq=tiles_per_seq),
        out_shape=jax.ShapeDtypeStruct((n, n_out), BF16),
        grid=(n_out // tn, n // tm),
        in_specs=[
            pl.BlockSpec((tm, d), lambda j, i: (i, 0)),
            pl.BlockSpec((CONV_HALO, d), lambda j, i: (jnp.maximum(i * halo_blocks - 1, 0), 0)),
            pl.BlockSpec((None, d, tn), lambda j, i: (layer, 0, j)),
            pl.BlockSpec((None, CONV_WIDTH, tn),
                         lambda j, i: (layer, 0, jnp.minimum(j, n_conv_tiles - 1))),
        ],
        out_specs=pl.BlockSpec((tm, tn), lambda j, i: (i, j)),
        compiler_params=_params(("parallel", "parallel")),
        name="proj_conv",
    )(x, x, w, conv_w)


def _out_proj_body(x_ref, w_ref, r_ref, o_ref, wb_ref):
    @pl.when(pl.program_id(1) == 0)
    def _():
        wb_ref[...] = w_ref[...].astype(BF16)

    o_ref[...] = r_ref[...] + _dot(x_ref[...], wb_ref[...])


def _out_proj(x, w, layer, res, *, tm=1024, tn=512):
    n, k = x.shape
    d = w.shape[2]
    tm, tn = _tile(n, tm), _tile(d, tn)
    return pl.pallas_call(
        _out_proj_body,
        out_shape=jax.ShapeDtypeStruct((n, d), F32),
        grid=(d // tn, n // tm),
        in_specs=[
            pl.BlockSpec((tm, k), lambda j, i: (i, 0)),
            pl.BlockSpec((None, k, tn), lambda j, i: (layer, 0, j)),
            pl.BlockSpec((tm, tn), lambda j, i: (i, j)),
        ],
        out_specs=pl.BlockSpec((tm, tn), lambda j, i: (i, j)),
        scratch_shapes=[pltpu.VMEM((k, tn), BF16)],
        compiler_params=_params(("parallel", "arbitrary")),
        name="out_proj",
    )(x, w, res)


def _diff_attn_body(lam_ref, subln_ref, q_ref, k_ref, v_ref, o_ref, *, tq, nq, lambda_init):
    i = pl.program_id(2)
    lv = lam_ref[...]
    lam = (jnp.exp(jnp.sum(lv[0:1] * lv[1:2], axis=-1, keepdims=True))
           - jnp.exp(jnp.sum(lv[2:3] * lv[3:4], axis=-1, keepdims=True)) + lambda_init)
    key_pos = lax.broadcasted_iota(jnp.int32, (tq, tq), 0)
    query_pos = lax.broadcasted_iota(jnp.int32, (tq, tq), 1)
    causal = key_pos <= query_pos

    def attend_both(n_full):
        diag = slice(n_full, n_full + tq)
        cols = [slice(sub * HEAD_DIM, (sub + 1) * HEAD_DIM) for sub in (0, 1)]
        qs = [q_ref[:, c] for c in cols]
        s_diag = [jnp.where(causal, _dot_nt(k_ref[diag, c], q), MASK_VALUE)
                  for c, q in zip(cols, qs)]
        s_full = [_dot_nt(k_ref[0:n_full, c], q) for c, q in zip(cols, qs)] if n_full else None
        outs = []
        for sub in (0, 1):
            m = jnp.max(s_diag[sub], axis=0, keepdims=True)
            if n_full:
                m = jnp.maximum(m, jnp.max(s_full[sub], axis=0, keepdims=True))
            p_diag = jnp.exp(s_diag[sub] - m)
            l = jnp.sum(p_diag, axis=0, keepdims=True)
            acc = _dot_tn(v_ref[diag, :], p_diag.astype(BF16))
            if n_full:
                p_full = jnp.exp(s_full[sub] - m)
                l = l + jnp.sum(p_full, axis=0, keepdims=True)
                acc = acc + _dot_tn(v_ref[0:n_full, :], p_full.astype(BF16))
            outs.append(acc / l)
        return outs

    for c in range(nq):
        @pl.when(i == c)
        def _(c=c):
            o0, o1 = attend_both(c * tq)
            o = o0 - lam * o1
            ms = jnp.mean(o * o, axis=0, keepdims=True)
            y = o * lax.rsqrt(ms + RMS_EPS) * subln_ref[...] * (1.0 - lambda_init)
            o_ref[...] = y.T.astype(o_ref.dtype)


def _diff_attention(qkv, lam_vecs, subln, batch, seq, lambda_init, *, tq=512):
    n, width = qkv.shape
    d = width // 3
    heads = d // DA_V_DIM
    tq = _tile(seq, tq)
    nq = seq // tq
    return pl.pallas_call(
        functools.partial(_diff_attn_body, tq=tq, nq=nq, lambda_init=lambda_init),
        out_shape=jax.ShapeDtypeStruct((n, d), BF16),
        grid=(batch, heads, nq),
        in_specs=[
            pl.BlockSpec((4, HEAD_DIM), lambda b, h, i: (0, 0)),
            pl.BlockSpec((DA_V_DIM, 1), lambda b, h, i: (0, 0)),
            pl.BlockSpec((tq, DA_V_DIM), lambda b, h, i: (b * nq + i, h)),
            pl.BlockSpec((seq, DA_V_DIM), lambda b, h, i: (b, heads + h)),
            pl.BlockSpec((seq, DA_V_DIM), lambda b, h, i: (b, 2 * heads + h)),
        ],
        out_specs=pl.BlockSpec((tq, DA_V_DIM), lambda b, h, i: (b * nq + i, h)),
        compiler_params=_params(("parallel", "parallel", "arbitrary")),
        name="diff_attn",
    )(lam_vecs, subln.reshape(DA_V_DIM, 1), qkv, qkv, qkv)


def _beta_decay_body(x_ref, w_ref, alog_ref, dt_ref, o_ref, ot_ref, *, n_heads):
    tm = x_ref.shape[0]
    y = _dot(x_ref[...], w_ref[...])
    a = y + dt_ref[...]
    softplus = jnp.maximum(a, 0.0) + jnp.log1p(jnp.exp(-jnp.abs(a)))
    g = -jnp.exp(alog_ref[...]) * softplus
    row = lax.broadcasted_iota(jnp.int32, (tm, tm), 0)
    col = lax.broadcasted_iota(jnp.int32, (tm, tm), 1)
    same_chunk = (row // CHUNK) == (col // CHUNK)
    prefix = jnp.where(same_chunk & (col <= row), 1.0, 0.0).astype(F32)
    whole = jnp.where(same_chunk, 1.0, 0.0).astype(F32)
    g_cum = _dot_exact(prefix, g)
    g_all = _dot_exact(whole, g)
    lane = lax.broadcasted_iota(jnp.int32, y.shape, 1)
    packed = jnp.where(lane < n_heads, jax.nn.sigmoid(y),
                       jnp.where(lane < 2 * n_heads, g_cum,
                                 jnp.where(lane < 3 * n_heads, g_all, 0.0)))
    o_ref[...] = packed
    ot_ref[...] = packed.T


def _beta_decay(x, w_b, w_a, a_log, dt_bias, *, tm=256):
    n, d = x.shape
    n_heads = a_log.shape[0]
    lanes = N_BETA_DECAY_LANES
    assert n % tm == 0 and tm % CHUNK == 0 and 3 * n_heads <= lanes
    pad = lanes - 3 * n_heads
    w = jnp.concatenate([w_b, w_a, w_a, jnp.zeros((d, pad), w_a.dtype)], axis=1).astype(BF16)
    zeros = jnp.zeros((n_heads,), F32)
    a_log_l = jnp.concatenate([zeros, a_log, a_log, jnp.zeros((pad,), F32)]).reshape(1, lanes)
    dt_l = jnp.concatenate([zeros, dt_bias, dt_bias, jnp.zeros((pad,), F32)]).reshape(1, lanes)
    return pl.pallas_call(
        functools.partial(_beta_decay_body, n_heads=n_heads),
        out_shape=(jax.ShapeDtypeStruct((n, lanes), F32), jax.ShapeDtypeStruct((lanes, n), F32)),
        grid=(n // tm,),
        in_specs=[
            pl.BlockSpec((tm, d), lambda i: (i, 0)),
            pl.BlockSpec((d, lanes), lambda i: (0, 0)),
            pl.BlockSpec((1, lanes), lambda i: (0, 0)),
            pl.BlockSpec((1, lanes), lambda i: (0, 0)),
        ],
        out_specs=(pl.BlockSpec((tm, lanes), lambda i: (i, 0)),
                   pl.BlockSpec((lanes, tm), lambda i: (0, i))),
        compiler_params=_params(("parallel",)),
        name="beta_decay",
    )(x, w, a_log_l, dt_l)


def _delta_body(q_ref, k_ref, v_ref, z_ref, bd_ref, bdt_ref, nw_ref, o_ref,
                state_ref, qeff_ref, oloc_ref, m_ref, n_ref, g_ref, *,
                tb, n_heads, rep, khs, n_groups):
    s = pl.program_id(1)
    n_pairs = pl.num_programs(1) - 1
    n_v = khs * rep
    n_chunks = tb // CHUNK
    lanes = N_BETA_DECAY_LANES
    prev_group = jnp.maximum(s - 1, 0) % n_groups
    hv0 = (jnp.minimum(s, n_pairs - 1) % n_groups) * n_v

    @pl.when(s == 0)
    def _():
        state_ref[...] = jnp.zeros_like(state_ref)
        qeff_ref[...] = jnp.zeros_like(qeff_ref)
        oloc_ref[...] = jnp.zeros_like(oloc_ref)
        m_ref[...] = jnp.zeros_like(m_ref)
        n_ref[...] = jnp.zeros_like(n_ref)
        g_ref[...] = jnp.zeros_like(g_ref)

    def head_slice(x, i):
        return x[:, i * HEAD_DIM:(i + 1) * HEAD_DIM]

    states = [state_ref[prev_group * n_v + idx] for idx in range(n_v)]
    outs = [[] for _ in range(n_v)]
    for c in range(n_chunks):
        rows = slice(c * CHUNK, (c + 1) * CHUNK)
        for idx in range(n_v):
            lhs = jnp.concatenate([m_ref[idx, c], qeff_ref[idx, rows, :]], axis=0)
            ms = _dot(lhs, states[idx].astype(BF16))
            outs[idx].append(ms[HEAD_DIM:] + oloc_ref[idx, rows, :])
            states[idx] = (states[idx] * g_ref[idx, c][0:1, :] - ms[:HEAD_DIM] + n_ref[idx, c])
    nw = nw_ref[...]
    for idx in range(n_v):
        state_ref[prev_group * n_v + idx] = states[idx]
        o = jnp.concatenate(outs[idx], axis=0)
        z = head_slice(z_ref, idx).astype(F32)
        o = o * lax.rsqrt(jnp.mean(o * o, axis=-1, keepdims=True) + RMS_EPS) * nw * _silu(z)
        o_ref[:, idx * HEAD_DIM:(idx + 1) * HEAD_DIM] = o.astype(o_ref.dtype)

    qs, ks, kks, qks = [], [], [], []
    for a in range(khs):
        q = head_slice(q_ref, a).astype(F32)
        k = head_slice(k_ref, a).astype(F32)
        q = q * lax.rsqrt(jnp.sum(q * q, axis=-1, keepdims=True) + L2_EPS) * (HEAD_DIM ** -0.5)
        k = k * lax.rsqrt(jnp.sum(k * k, axis=-1, keepdims=True) + L2_EPS)
        qs.append(q)
        ks.append(k)
        k_b = k.astype(BF16)
        kks.append(_dot_nt(k_b, k_b))
        qks.append(_dot_nt(q.astype(BF16), k_b))

    row = lax.broadcasted_iota(jnp.int32, (tb, tb), 0)
    col = lax.broadcasted_iota(jnp.int32, (tb, tb), 1)
    same_chunk = (row // CHUNK) == (col // CHUNK)
    tril = same_chunk & (col <= row)
    strict = same_chunk & (col < row)

    bd = pltpu.roll(bd_ref[...], (lanes - hv0) % lanes, 1)

    p_cur, rhs, qk_d, q_g, k_g, g_chunk = [], [], [], [], [], []
    for idx in range(n_v):
        a = idx // rep
        beta_c = bd[:, idx:idx + 1]
        gcum_c = bd[:, n_heads + idx:n_heads + idx + 1]
        gall_c = bd[:, 2 * n_heads + idx:2 * n_heads + idx + 1]
        gcum_r = bdt_ref[pl.ds(n_heads + hv0 + idx, 1), :]
        e = jnp.exp(gcum_c - gcum_r)
        p_cur.append((kks[a] * jnp.where(strict, e, 0.0) * (-beta_c)).astype(BF16))
        qk_d.append((qks[a] * jnp.where(tril, e, 0.0)).astype(BF16))
        exp_g = jnp.exp(gcum_c)
        rhs.append(jnp.concatenate([head_slice(v_ref, idx).astype(F32) * beta_c,
                                    ks[a] * (beta_c * exp_g)], axis=-1))
        q_g.append(qs[a] * exp_g)
        k_g.append((ks[a] * jnp.exp(gall_c - gcum_c)).astype(BF16))
        g_chunk.append(jnp.exp(gall_c))

    n_factors = CHUNK.bit_length() - 1
    for j in range(n_factors):
        for idx in range(n_v):
            p_b = p_cur[idx]
            r_b = rhs[idx].astype(BF16)
            if j < n_factors - 1:
                res = _dot(p_b, jnp.concatenate([p_b, r_b], axis=-1))
                p_cur[idx] = res[:, :tb].astype(BF16)
                rhs[idx] = rhs[idx] + res[:, tb:]
            else:
                rhs[idx] = rhs[idx] + _dot(p_b, r_b)

    for idx in range(n_v):
        uw_b = rhs[idx].astype(BF16)
        quw = _dot(qk_d[idx], uw_b)
        oloc_ref[idx] = quw[:, :HEAD_DIM]
        qeff_ref[idx] = (q_g[idx] - quw[:, HEAD_DIM:]).astype(BF16)
        for c in range(n_chunks):
            rows = slice(c * CHUNK, (c + 1) * CHUNK)
            kuw = _dot_tn(k_g[idx][rows], uw_b[rows])
            n_ref[idx, c] = kuw[:, :HEAD_DIM]
            m_ref[idx, c] = kuw[:, HEAD_DIM:].astype(BF16)
            g_ref[idx, c] = jnp.broadcast_to(g_chunk[idx][c * CHUNK:c * CHUNK + 1, :],
                                             (SUBLANES, HEAD_DIM))


def _gated_delta(proj, bd, bd_t, norm_w, batch, seq, n_k_heads, n_v_heads, *, tb=256, khs=4):
    n = proj.shape[0]
    rep = n_v_heads // n_k_heads
    khs = _tile(n_k_heads, khs)
    n_groups = n_k_heads // khs
    n_v = khs * rep
    kw = khs * HEAD_DIM
    vw = n_v * HEAD_DIM
    key_dim = n_k_heads * HEAD_DIM
    val_dim = n_v_heads * HEAD_DIM
    assert seq % tb == 0 and tb % CHUNK == 0 and tb % HEAD_DIM == 0
    nt = seq // tb
    n_chunks = tb // CHUNK
    n_pairs = nt * n_groups
    k_blk = key_dim // kw
    v_blk = 2 * key_dim // vw
    z_blk = (2 * key_dim + val_dim) // vw
    lanes = N_BETA_DECAY_LANES

    def prep(s):
        pair = jnp.minimum(s, n_pairs - 1)
        return pair // n_groups, pair % n_groups

    def fin(s):
        pair = jnp.maximum(s - 1, 0)
        return pair // n_groups, pair % n_groups

    return pl.pallas_call(
        functools.partial(_delta_body, tb=tb, n_heads=n_v_heads, rep=rep, khs=khs,
                          n_groups=n_groups),
        out_shape=jax.ShapeDtypeStruct((n, val_dim), BF16),
        grid=(batch, n_pairs + 1),
        in_specs=[
            pl.BlockSpec((tb, kw), lambda b, s: (b * nt + prep(s)[0], prep(s)[1])),
            pl.BlockSpec((tb, kw), lambda b, s: (b * nt + prep(s)[0], k_blk + prep(s)[1])),
            pl.BlockSpec((tb, vw), lambda b, s: (b * nt + prep(s)[0], v_blk + prep(s)[1])),
            pl.BlockSpec((tb, vw), lambda b, s: (b * nt + fin(s)[0], z_blk + fin(s)[1])),
            pl.BlockSpec((tb, lanes), lambda b, s: (b * nt + prep(s)[0], 0)),
            pl.BlockSpec((lanes, tb), lambda b, s: (0, b * nt + prep(s)[0])),
            pl.BlockSpec((1, HEAD_DIM), lambda b, s: (0, 0)),
        ],
        out_specs=pl.BlockSpec((tb, vw), lambda b, s: (b * nt + fin(s)[0], fin(s)[1])),
        scratch_shapes=[
            pltpu.VMEM((n_v_heads, HEAD_DIM, HEAD_DIM), F32),
            pltpu.VMEM((n_v, tb, HEAD_DIM), BF16),
            pltpu.VMEM((n_v, tb, HEAD_DIM), F32),
            pltpu.VMEM((n_v, n_chunks, HEAD_DIM, HEAD_DIM), BF16),
            pltpu.VMEM((n_v, n_chunks, HEAD_DIM, HEAD_DIM), F32),
            pltpu.VMEM((n_v, n_chunks, SUBLANES, HEAD_DIM), F32),
        ],
        compiler_params=_params(("parallel", "arbitrary")),
        name="gated_delta",
    )(proj, proj, proj, proj, bd, bd_t, norm_w.reshape(1, HEAD_DIM))


def _rope_tables(seq):
    inv = 1.0 / (ROPE_THETA ** (jnp.arange(0, HEAD_DIM, 2, dtype=F32) / HEAD_DIM))
    ang = jnp.arange(seq, dtype=F32)[:, None] * inv[None, :]
    ang = jnp.concatenate([ang, ang], axis=-1)
    sign = jnp.where(jnp.arange(HEAD_DIM) < HEAD_DIM // 2, -1.0, 1.0).astype(F32)
    return jnp.cos(ang), jnp.sin(ang) * sign[None, :]


def _diff_attention_mixer(h, xn, w_qkv, layer, lq1, lk1, lq2, lk2, subln, w_o, lambda_init,
                          batch, seq):
    d = h.shape[1]
    cos, sin = _rope_tables(seq)
    qkv = _proj_rope(xn, w_qkv, layer, cos, sin, seq, d, 2 * d, HEAD_DIM ** -0.5)
    lam_vecs = jnp.stack([lq1, lk1, lq2, lk2]).astype(F32)
    attn = _diff_attention(qkv, lam_vecs, subln, batch, seq, lambda_init)
    return _out_proj(attn, w_o, layer, h)


def _gated_deltanet_mixer(h, xn, w_in, conv_w, layer, a_log, dt_bias, gdn_norm, w_o, batch, seq):
    n_v_heads = a_log.shape[0]
    val_dim = n_v_heads * HEAD_DIM
    conv_dim = conv_w.shape[2]
    key_dim = (conv_dim - val_dim) // 2
    n_k_heads = key_dim // HEAD_DIM
    main = conv_dim + val_dim
    proj = _proj_conv(xn, w_in.astype(BF16), conv_w, layer, seq, main)
    bd, bd_t = _beta_decay(xn, w_in[layer, :, main:main + n_v_heads],
                           w_in[layer, :, main + n_v_heads:], a_log, dt_bias)
    o = _gated_delta(proj, bd, bd_t, gdn_norm, batch, seq, n_k_heads, n_v_heads)
    return _out_proj(o, w_o, layer, h)


def kernel(x, ffn1_norm, ffn1_w_gate, ffn1_w_up, ffn1_w_down, mix_norm, ffn2_norm, ffn2_w_gate, ffn2_w_up, ffn2_w_down, da_w_qkv, da_lambda_q1, da_lambda_k1, da_lambda_q2, da_lambda_k2, da_subln, da_w_o, gdn_w_in, gdn_conv_w, gdn_a_log, gdn_dt_bias, gdn_norm, gdn_w_o, final_norm):
    batch, seq, d = x.shape
    depth = ffn1_norm.shape[0]
    h = x.reshape(batch * seq, d)
    for i in range(depth):
        h, xn = _ffn(h, ffn1_norm[i], ffn1_w_gate, ffn1_w_up, ffn1_w_down, i, "next", mix_norm[i])
        j = i // 2
        if i % 2 == 0:
            lambda_init = 0.8 - 0.6 * math.exp(-0.3 * i)
            h = _diff_attention_mixer(h, xn, da_w_qkv, j, da_lambda_q1[j], da_lambda_k1[j],
                                      da_lambda_q2[j], da_lambda_k2[j], da_subln[j], da_w_o,
                                      lambda_init, batch, seq)
        else:
            h = _gated_deltanet_mixer(h, xn, gdn_w_in, gdn_conv_w, j, gdn_a_log[j],
                                      gdn_dt_bias[j], gdn_norm[j], gdn_w_o, batch, seq)
        if i == depth - 1:
            h = _ffn(h, ffn2_norm[i], ffn2_w_gate, ffn2_w_up, ffn2_w_down, i, "final", final_norm)
        else:
            h = _ffn(h, ffn2_norm[i], ffn2_w_gate, ffn2_w_up, ffn2_w_down, i)
    return h.reshape(batch, seq, d)
```

```python
import functools
import math

import jax
import jax.numpy as jnp
from jax import lax
from jax.experimental import pallas as pl
from jax.experimental.pallas import tpu as pltpu

F32 = jnp.float32
BF16 = jnp.bfloat16

RMS_EPS = 1e-6
L2_EPS = 1e-6
ROPE_THETA = 10000.0
HEAD_DIM = 128
DA_V_DIM = 2 * HEAD_DIM
CHUNK = 64
CONV_WIDTH = 4
CONV_HALO = 16
SUBLANES = 8
MXU_COLS = 256
N_BETA_DECAY_LANES = 128
VMEM_LIMIT = 56 * 1024 * 1024
MASK_VALUE = -1e30


def _params(semantics):
    return pltpu.CompilerParams(dimension_semantics=semantics, vmem_limit_bytes=VMEM_LIMIT)


def _tile(total, preferred):
    tile = min(preferred, total)
    while total % tile:
        tile //= 2
    return tile


def _rms_normalize(x, w):
    ms = jnp.mean(x * x, axis=-1, keepdims=True)
    return x * lax.rsqrt(ms + RMS_EPS) * w


def _silu(x):
    return x * jax.nn.sigmoid(x)


def _dot(a, b):
    return jnp.dot(a, b, preferred_element_type=F32)


def _dot_nt(a, b):
    return lax.dot_general(a, b, (((1,), (1,)), ((), ())), preferred_element_type=F32)


def _dot_tn(a, b):
    return lax.dot_general(a, b, (((0,), (0,)), ((), ())), preferred_element_type=F32)


def _dot_exact(a, b):
    return jnp.dot(a, b, preferred_element_type=F32, precision=lax.Precision.HIGHEST)


def _ffn_body(h_ref, nw_ref, wg_ref, wu_ref, wd_ref, ew_ref, o_ref, xn_ref, *, epilogue):
    j = pl.program_id(1)

    @pl.when(j == 0)
    def _():
        xn_ref[...] = _rms_normalize(h_ref[...], nw_ref[...]).astype(BF16)
        o_ref[...] = jnp.zeros_like(o_ref)

    xn = xn_ref[...]
    g = _dot(xn, wg_ref[...].astype(BF16))
    u = _dot(xn, wu_ref[...].astype(BF16))
    a = (_silu(g) * u).astype(BF16)
    o_ref[...] += _dot(a, wd_ref[...].astype(BF16))

    @pl.when(j == pl.num_programs(1) - 1)
    def _():
        y = h_ref[...] + 0.5 * o_ref[...]
        if epilogue == "final":
            o_ref[...] = _rms_normalize(y, ew_ref[...])
        else:
            o_ref[...] = y
            if epilogue == "next":
                xn_ref[...] = _rms_normalize(y, ew_ref[...]).astype(BF16)


def _ffn(h, norm_w, w_gate, w_up, w_down, layer, epilogue=None, epilogue_w=None, *,
         tm=1024, tf=256):
    n, d = h.shape
    f = w_gate.shape[2]
    tm, tf = _tile(n, tm), _tile(f, tf)
    ew = norm_w if epilogue_w is None else epilogue_w
    row_block = pl.BlockSpec((tm, d), lambda i, j: (i, 0))
    out_shape = jax.ShapeDtypeStruct((n, d), F32)
    out_specs = row_block
    scratch_shapes = [pltpu.VMEM((tm, d), BF16)]
    if epilogue == "next":
        out_shape = (out_shape, jax.ShapeDtypeStruct((n, d), BF16))
        out_specs = (row_block, row_block)
        scratch_shapes = []
    return pl.pallas_call(
        functools.partial(_ffn_body, epilogue=epilogue),
        out_shape=out_shape,
        grid=(n // tm, f // tf),
        in_specs=[
            pl.BlockSpec((tm, d), lambda i, j: (i, 0), pipeline_mode=pl.Buffered(1)),
            pl.BlockSpec((1, d), lambda i, j: (0, 0)),
            pl.BlockSpec((None, d, tf), lambda i, j: (layer, 0, j)),
            pl.BlockSpec((None, d, tf), lambda i, j: (layer, 0, j)),
            pl.BlockSpec((None, tf, d), lambda i, j: (layer, j, 0)),
            pl.BlockSpec((1, d), lambda i, j: (0, 0)),
        ],
        out_specs=out_specs,
        scratch_shapes=scratch_shapes,
        compiler_params=_params(("parallel", "arbitrary")),
        name="ffn",
    )(h, norm_w.reshape(1, d), w_gate, w_up, w_down, ew.reshape(1, d))


def _proj_rope_body(x_ref, w_ref, cos_ref, sin_ref, o_ref, wb_ref, *,
                    n_q_tiles, n_rope_tiles, q_scale):
    j = pl.program_id(0)

    @pl.when(pl.program_id(1) == 0)
    def _():
        wb_ref[...] = w_ref[...].astype(BF16)

    tn = o_ref.shape[1]

    @pl.when(j < n_rope_tiles)
    def _():
        cos = cos_ref[...]
        sin = sin_ref[...]
        scale = jnp.where(j < n_q_tiles, q_scale, 1.0).astype(F32)
        for g in range(tn // MXU_COLS):
            y = _dot(x_ref[...], wb_ref[:, g * MXU_COLS:(g + 1) * MXU_COLS])
            for c in range(MXU_COLS // HEAD_DIM):
                x = y[:, c * HEAD_DIM:(c + 1) * HEAD_DIM]
                swapped = pltpu.roll(x, HEAD_DIM // 2, 1)
                lo = g * MXU_COLS + c * HEAD_DIM
                o_ref[:, lo:lo + HEAD_DIM] = (
                    (x * cos + swapped * sin) * scale).astype(o_ref.dtype)

    @pl.when(j >= n_rope_tiles)
    def _():
        o_ref[...] = _dot(x_ref[...], wb_ref[...]).astype(o_ref.dtype)


def _proj_rope(x, w, layer, cos, sin, seq, n_q_cols, n_rope_cols, q_scale, *, tm=1024, tn=1024):
    n, d = x.shape
    n_out = w.shape[2]
    tm, tn = _tile(seq, tm), _tile(n_q_cols, tn)
    assert n % tm == 0 and n_out % tn == 0 and n_rope_cols % tn == 0
    tiles_per_seq = seq // tm
    return pl.pallas_call(
        functools.partial(_proj_rope_body, n_q_tiles=n_q_cols // tn,
                          n_rope_tiles=n_rope_cols // tn, q_scale=q_scale),
        out_shape=jax.ShapeDtypeStruct((n, n_out), BF16),
        grid=(n_out // tn, n // tm),
        in_specs=[
            pl.BlockSpec((tm, d), lambda j, i: (i, 0)),
            pl.BlockSpec((None, d, tn), lambda j, i: (layer, 0, j)),
            pl.BlockSpec((tm, HEAD_DIM), lambda j, i: (i % tiles_per_seq, 0)),
            pl.BlockSpec((tm, HEAD_DIM), lambda j, i: (i % tiles_per_seq, 0)),
        ],
        out_specs=pl.BlockSpec((tm, tn), lambda j, i: (i, j)),
        scratch_shapes=[pltpu.VMEM((d, tn), BF16)],
        compiler_params=_params(("parallel", "arbitrary")),
        name="proj_rope",
    )(x, w, cos, sin)


def _proj_conv_body(x_ref, xh_ref, w_ref, cw_ref, o_ref, *, n_conv_tiles, tiles_per_seq):
    j = pl.program_id(0)
    i = pl.program_id(1)
    tm = x_ref.shape[0]
    keep = jnp.where(i % tiles_per_seq == 0, 0.0, 1.0).astype(BF16)

    @pl.when(j < n_conv_tiles)
    def _():
        lhs = jnp.concatenate([xh_ref[...] * keep, x_ref[...]], axis=0)
        for g in range(o_ref.shape[1] // MXU_COLS):
            cols = slice(g * MXU_COLS, (g + 1) * MXU_COLS)
            y = _dot(lhs, w_ref[:, cols])
            cw = cw_ref[:, cols]
            groups = y.shape[0] // SUBLANES
            halo_groups = CONV_HALO // SUBLANES
            y3 = y.reshape(groups, SUBLANES, MXU_COLS)
            sub = lax.broadcasted_iota(jnp.int32, (groups - halo_groups, SUBLANES, MXU_COLS), 1)
            acc = cw[CONV_WIDTH - 1:CONV_WIDTH, :][None] * y3[halo_groups:]
            rot = y3
            for back in range(1, CONV_WIDTH):
                tap = CONV_WIDTH - 1 - back
                rot = pltpu.roll(rot, 1, 1)
                shifted = jnp.where(sub >= back, rot[halo_groups:], rot[halo_groups - 1:-1])
                acc = acc + cw[tap:tap + 1, :][None] * shifted
            o_ref[:, cols] = _silu(acc).reshape(tm, MXU_COLS).astype(o_ref.dtype)

    @pl.when(j >= n_conv_tiles)
    def _():
        o_ref[...] = _dot(x_ref[...], w_ref[...]).astype(o_ref.dtype)


def _proj_conv(x, w, conv_w, layer, seq, n_out, *, tm=1024, tn=2048):
    assert w.dtype == BF16
    n, d = x.shape
    n_conv = conv_w.shape[2]
    tm, tn = _tile(seq, tm), _tile(math.gcd(n_conv, n_out), tn)
    assert n % tm == 0 and tm % CONV_HALO == 0
    assert CONV_WIDTH - 1 < SUBLANES <= CONV_HALO and CONV_HALO % SUBLANES == 0
    tiles_per_seq = seq // tm
    n_conv_tiles = n_conv // tn
    halo_blocks = tm // CONV_HALO
    return pl.pallas_call(
        functools.partial(_proj_conv_body, n_conv_tiles=n_conv_tiles,
                          tiles_per_seq=tiles_per_seq),
        out_shape=jax.ShapeDtypeStruct((n, n_out), BF16),
        grid=(n_out // tn, n // tm),
        in_specs=[
            pl.BlockSpec((tm, d), lambda j, i: (i, 0)),
            pl.BlockSpec((CONV_HALO, d), lambda j, i: (jnp.maximum(i * halo_blocks - 1, 0), 0)),
            pl.BlockSpec((None, d, tn), lambda j, i: (layer, 0, j)),
            pl.BlockSpec((None, CONV_WIDTH, tn),
                         lambda j, i: (layer, 0, jnp.minimum(j, n_conv_tiles - 1))),
        ],
        out_specs=pl.BlockSpec((tm, tn), lambda j, i: (i, j)),
        compiler_params=_params(("parallel", "parallel")),
        name="proj_conv",
    )(x, x, w, conv_w)


def _out_proj_body(x_ref, w_ref, r_ref, o_ref, wb_ref):
    @pl.when(pl.program_id(1) == 0)
    def _():
        wb_ref[...] = w_ref[...].astype(BF16)

    o_ref[...] = r_ref[...] + _dot(x_ref[...], wb_ref[...])


def _out_proj(x, w, layer, res, *, tm=1024, tn=512):
    n, k = x.shape
    d = w.shape[2]
    tm, tn = _tile(n, tm), _tile(d, tn)
    return pl.pallas_call(
        _out_proj_body,
        out_shape=jax.ShapeDtypeStruct((n, d), F32),
        grid=(d // tn, n // tm),
        in_specs=[
            pl.BlockSpec((tm, k), lambda j, i: (i, 0)),
            pl.BlockSpec((None, k, tn), lambda j, i: (layer, 0, j)),
            pl.BlockSpec((tm, tn), lambda j, i: (i, j)),
        ],
        out_specs=pl.BlockSpec((tm, tn), lambda j, i: (i, j)),
        scratch_shapes=[pltpu.VMEM((k, tn), BF16)],
        compiler_params=_params(("parallel", "arbitrary")),
        name="out_proj",
    )(x, w, res)


def _diff_attn_body(lam_ref, subln_ref, q_ref, k_ref, v_ref, o_ref, *, tq, nq, lambda_init):
    i = pl.program_id(2)
    lv = lam_ref[...]
    lam = (jnp.exp(jnp.sum(lv[0:1] * lv[1:2], axis=-1, keepdims=True))
           - jnp.exp(jnp.sum(lv[2:3] * lv[3:4], axis=-1, keepdims=True)) + lambda_init)
    key_pos = lax.broadcasted_iota(jnp.int32, (tq, tq), 0)
    query_pos = lax.broadcasted_iota(jnp.int32, (tq, tq), 1)
    causal = key_pos <= query_pos

    def attend_both(n_full):
        diag = slice(n_full, n_full + tq)
        cols = [slice(sub * HEAD_DIM, (sub + 1) * HEAD_DIM) for sub in (0, 1)]
        qs = [q_ref[:, c] for c in cols]
        s_diag = [jnp.where(causal, _dot_nt(k_ref[diag, c], q), MASK_VALUE)
                  for c, q in zip(cols, qs)]
        s_full = [_dot_nt(k_ref[0:n_full, c], q) for c, q in zip(cols, qs)] if n_full else None
        outs = []
        for sub in (0, 1):
            m = jnp.max(s_diag[sub], axis=0, keepdims=True)
            if n_full:
                m = jnp.maximum(m, jnp.max(s_full[sub], axis=0, keepdims=True))
            p_diag = jnp.exp(s_diag[sub] - m)
            l = jnp.sum(p_diag, axis=0, keepdims=True)
            acc = _dot_tn(v_ref[diag, :], p_diag.astype(BF16))
            if n_full:
                p_full = jnp.exp(s_full[sub] - m)
                l = l + jnp.sum(p_full, axis=0, keepdims=True)
                acc = acc + _dot_tn(v_ref[0:n_full, :], p_full.astype(BF16))
            outs.append(acc / l)
        return outs

    for c in range(nq):
        @pl.when(i == c)
        def _(c=c):
            o0, o1 = attend_both(c * tq)
            o = o0 - lam * o1
            ms = jnp.mean(o * o, axis=0, keepdims=True)
            y = o * lax.rsqrt(ms + RMS_EPS) * subln_ref[...] * (1.0 - lambda_init)
            o_ref[...] = y.T.astype(o_ref.dtype)


def _diff_attention(qkv, lam_vecs, subln, batch, seq, lambda_init, *, tq=512):
    n, width = qkv.shape
    d = width // 3
    heads = d // DA_V_DIM
    tq = _tile(seq, tq)
    nq = seq // tq
    return pl.pallas_call(
        functools.partial(_diff_attn_body, tq=tq, nq=nq, lambda_init=lambda_init),
        out_shape=jax.ShapeDtypeStruct((n, d), BF16),
        grid=(batch, heads, nq),
        in_specs=[
            pl.BlockSpec((4, HEAD_DIM), lambda b, h, i: (0, 0)),
            pl.BlockSpec((DA_V_DIM, 1), lambda b, h, i: (0, 0)),
            pl.BlockSpec((tq, DA_V_DIM), lambda b, h, i: (b * nq + i, h)),
            pl.BlockSpec((seq, DA_V_DIM), lambda b, h, i: (b, heads + h)),
            pl.BlockSpec((seq, DA_V_DIM), lambda b, h, i: (b, 2 * heads + h)),
        ],
        out_specs=pl.BlockSpec((tq, DA_V_DIM), lambda b, h, i: (b * nq + i, h)),
        compiler_params=_params(("parallel", "parallel", "arbitrary")),
        name="diff_attn",
    )(lam_vecs, subln.reshape(DA_V_DIM, 1), qkv, qkv, qkv)


def _beta_decay_body(x_ref, w_ref, alog_ref, dt_ref, o_ref, ot_ref, *, n_heads):
    tm = x_ref.shape[0]
    y = _dot(x_ref[...], w_ref[...])
    a = y + dt_ref[...]
    softplus = jnp.maximum(a, 0.0) + jnp.log1p(jnp.exp(-jnp.abs(a)))
    g = -jnp.exp(alog_ref[...]) * softplus
    row = lax.broadcasted_iota(jnp.int32, (tm, tm), 0)
    col = lax.broadcasted_iota(jnp.int32, (tm, tm), 1)
    same_chunk = (row // CHUNK) == (col // CHUNK)
    prefix = jnp.where(same_chunk & (col <= row), 1.0, 0.0).astype(F32)
    whole = jnp.where(same_chunk, 1.0, 0.0).astype(F32)
    g_cum = _dot_exact(prefix, g)
    g_all = _dot_exact(whole, g)
    lane = lax.broadcasted_iota(jnp.int32, y.shape, 1)
    packed = jnp.where(lane < n_heads, jax.nn.sigmoid(y),
                       jnp.where(lane < 2 * n_heads, g_cum,
                                 jnp.where(lane < 3 * n_heads, g_all, 0.0)))
    o_ref[...] = packed
    ot_ref[...] = packed.T


def _beta_decay(x, w_b, w_a, a_log, dt_bias, *, tm=256):
    n, d = x.shape
    n_heads = a_log.shape[0]
    lanes = N_BETA_DECAY_LANES
    assert n % tm == 0 and tm % CHUNK == 0 and 3 * n_heads <= lanes
    pad = lanes - 3 * n_heads
    w = jnp.concatenate([w_b, w_a, w_a, jnp.zeros((d, pad), w_a.dtype)], axis=1).astype(BF16)
    zeros = jnp.zeros((n_heads,), F32)
    a_log_l = jnp.concatenate([zeros, a_log, a_log, jnp.zeros((pad,), F32)]).reshape(1, lanes)
    dt_l = jnp.concatenate([zeros, dt_bias, dt_bias, jnp.zeros((pad,), F32)]).reshape(1, lanes)
    return pl.pallas_call(
        functools.partial(_beta_decay_body, n_heads=n_heads),
        out_shape=(jax.ShapeDtypeStruct((n, lanes), F32), jax.ShapeDtypeStruct((lanes, n), F32)),
        grid=(n // tm,),
        in_specs=[
            pl.BlockSpec((tm, d), lambda i: (i, 0)),
            pl.BlockSpec((d, lanes), lambda i: (0, 0)),
            pl.BlockSpec((1, lanes), lambda i: (0, 0)),
            pl.BlockSpec((1, lanes), lambda i: (0, 0)),
        ],
        out_specs=(pl.BlockSpec((tm, lanes), lambda i: (i, 0)),
                   pl.BlockSpec((lanes, tm), lambda i: (0, i))),
        compiler_params=_params(("parallel",)),
        name="beta_decay",
    )(x, w, a_log_l, dt_l)


def _delta_body(q_ref, k_ref, v_ref, z_ref, bd_ref, bdt_ref, nw_ref, o_ref,
                state_ref, qeff_ref, oloc_ref, m_ref, n_ref, g_ref, *,
                tb, n_heads, rep, khs, n_groups):
    s = pl.program_id(1)
    n_pairs = pl.num_programs(1) - 1
    n_v = khs * rep
    n_chunks = tb // CHUNK
    lanes = N_BETA_DECAY_LANES
    prev_group = jnp.maximum(s - 1, 0) % n_groups
    hv0 = (jnp.minimum(s, n_pairs - 1) % n_groups) * n_v

    @pl.when(s == 0)
    def _():
        state_ref[...] = jnp.zeros_like(state_ref)
        qeff_ref[...] = jnp.zeros_like(qeff_ref)
        oloc_ref[...] = jnp.zeros_like(oloc_ref)
        m_ref[...] = jnp.zeros_like(m_ref)
        n_ref[...] = jnp.zeros_like(n_ref)
        g_ref[...] = jnp.zeros_like(g_ref)

    def head_slice(x, i):
        return x[:, i * HEAD_DIM:(i + 1) * HEAD_DIM]

    states = [state_ref[prev_group * n_v + idx] for idx in range(n_v)]
    outs = [[] for _ in range(n_v)]
    for c in range(n_chunks):
        rows = slice(c * CHUNK, (c + 1) * CHUNK)
        for idx in range(n_v):
            lhs = jnp.concatenate([m_ref[idx, c], qeff_ref[idx, rows, :]], axis=0)
            ms = _dot(lhs, states[idx].astype(BF16))
            outs[idx].append(ms[HEAD_DIM:] + oloc_ref[idx, rows, :])
            states[idx] = (states[idx] * g_ref[idx, c][0:1, :] - ms[:HEAD_DIM] + n_ref[idx, c])
    nw = nw_ref[...]
    for idx in range(n_v):
        state_ref[prev_group * n_v + idx] = states[idx]
        o = jnp.concatenate(outs[idx], axis=0)
        z = head_slice(z_ref, idx).astype(F32)
        o = o * lax.rsqrt(jnp.mean(o * o, axis=-1, keepdims=True) + RMS_EPS) * nw * _silu(z)
        o_ref[:, idx * HEAD_DIM:(idx + 1) * HEAD_DIM] = o.astype(o_ref.dtype)

    qs, ks, kks, qks = [], [], [], []
    for a in range(khs):
        q = head_slice(q_ref, a).astype(F32)
        k = head_slice(k_ref, a).astype(F32)
        q = q * lax.rsqrt(jnp.sum(q * q, axis=-1, keepdims=True) + L2_EPS) * (HEAD_DIM ** -0.5)
        k = k * lax.rsqrt(jnp.sum(k * k, axis=-1, keepdims=True) + L2_EPS)
        qs.append(q)
        ks.append(k)
        k_b = k.astype(BF16)
        kks.append(_dot_nt(k_b, k_b))
        qks.append(_dot_nt(q.astype(BF16), k_b))

    row = lax.broadcasted_iota(jnp.int32, (tb, tb), 0)
    col = lax.broadcasted_iota(jnp.int32, (tb, tb), 1)
    same_chunk = (row // CHUNK) == (col // CHUNK)
    tril = same_chunk & (col <= row)
    strict = same_chunk & (col < row)

    bd = pltpu.roll(bd_ref[...], (lanes - hv0) % lanes, 1)

    p_cur, rhs, qk_d, q_g, k_g, g_chunk = [], [], [], [], [], []
    for idx in range(n_v):
        a = idx // rep
        beta_c = bd[:, idx:idx + 1]
        gcum_c = bd[:, n_heads + idx:n_heads + idx + 1]
        gall_c = bd[:, 2 * n_heads + idx:2 * n_heads + idx + 1]
        gcum_r = bdt_ref[pl.ds(n_heads + hv0 + idx, 1), :]
        e = jnp.exp(gcum_c - gcum_r)
        p_cur.append((kks[a] * jnp.where(strict, e, 0.0) * (-beta_c)).astype(BF16))
        qk_d.append((qks[a] * jnp.where(tril, e, 0.0)).astype(BF16))
        exp_g = jnp.exp(gcum_c)
        rhs.append(jnp.concatenate([head_slice(v_ref, idx).astype(F32) * beta_c,
                                    ks[a] * (beta_c * exp_g)], axis=-1))
        q_g.append(qs[a] * exp_g)
        k_g.append((ks[a] * jnp.exp(gall_c - gcum_c)).astype(BF16))
        g_chunk.append(jnp.exp(gall_c))

    n_factors = CHUNK.bit_length() - 1
    blk = HEAD_DIM
    n_blk = tb // blk
    p_cur = [[p[h * blk:(h + 1) * blk, h * blk:(h + 1) * blk] for h in range(n_blk)]
             for p in p_cur]
    rhs = [[r[h * blk:(h + 1) * blk] for h in range(n_blk)] for r in rhs]
    for j in range(n_factors):
        for idx in range(n_v):
            for h in range(n_blk):
                p_b = p_cur[idx][h]
                r_b = rhs[idx][h].astype(BF16)
                if j < n_factors - 1:
                    res = _dot(p_b, jnp.concatenate([p_b, r_b], axis=-1))
                    p_cur[idx][h] = res[:, :blk].astype(BF16)
                    rhs[idx][h] = rhs[idx][h] + res[:, blk:]
                else:
                    rhs[idx][h] = rhs[idx][h] + _dot(p_b, r_b)
    rhs = [jnp.concatenate(r, axis=0) for r in rhs]

    for idx in range(n_v):
        uw_b = rhs[idx].astype(BF16)
        quw = _dot(qk_d[idx], uw_b)
        oloc_ref[idx] = quw[:, :HEAD_DIM]
        qeff_ref[idx] = (q_g[idx] - quw[:, HEAD_DIM:]).astype(BF16)
        for c in range(n_chunks):
            rows = slice(c * CHUNK, (c + 1) * CHUNK)
            kuw = _dot_tn(k_g[idx][rows], uw_b[rows])
            n_ref[idx, c] = kuw[:, :HEAD_DIM]
            m_ref[idx, c] = kuw[:, HEAD_DIM:].astype(BF16)
            g_ref[idx, c] = jnp.broadcast_to(g_chunk[idx][c * CHUNK:c * CHUNK + 1, :],
                                             (SUBLANES, HEAD_DIM))


def _gated_delta(proj, bd, bd_t, norm_w, batch, seq, n_k_heads, n_v_heads, *, tb=256, khs=4):
    n = proj.shape[0]
    rep = n_v_heads // n_k_heads
    khs = _tile(n_k_heads, khs)
    n_groups = n_k_heads // khs
    n_v = khs * rep
    kw = khs * HEAD_DIM
    vw = n_v * HEAD_DIM
    key_dim = n_k_heads * HEAD_DIM
    val_dim = n_v_heads * HEAD_DIM
    assert seq % tb == 0 and tb % HEAD_DIM == 0 and HEAD_DIM % CHUNK == 0
    nt = seq // tb
    n_chunks = tb // CHUNK
    n_pairs = nt * n_groups
    k_blk = key_dim // kw
    v_blk = 2 * key_dim // vw
    z_blk = (2 * key_dim + val_dim) // vw
    lanes = N_BETA_DECAY_LANES

    def prep(s):
        pair = jnp.minimum(s, n_pairs - 1)
        return pair // n_groups, pair % n_groups

    def fin(s):
        pair = jnp.maximum(s - 1, 0)
        return pair // n_groups, pair % n_groups

    return pl.pallas_call(
        functools.partial(_delta_body, tb=tb, n_heads=n_v_heads, rep=rep, khs=khs,
                          n_groups=n_groups),
        out_shape=jax.ShapeDtypeStruct((n, val_dim), BF16),
        grid=(batch, n_pairs + 1),
        in_specs=[
            pl.BlockSpec((tb, kw), lambda b, s: (b * nt + prep(s)[0], prep(s)[1])),
            pl.BlockSpec((tb, kw), lambda b, s: (b * nt + prep(s)[0], k_blk + prep(s)[1])),
            pl.BlockSpec((tb, vw), lambda b, s: (b * nt + prep(s)[0], v_blk + prep(s)[1])),
            pl.BlockSpec((tb, vw), lambda b, s: (b * nt + fin(s)[0], z_blk + fin(s)[1])),
            pl.BlockSpec((tb, lanes), lambda b, s: (b * nt + prep(s)[0], 0)),
            pl.BlockSpec((lanes, tb), lambda b, s: (0, b * nt + prep(s)[0])),
            pl.BlockSpec((1, HEAD_DIM), lambda b, s: (0, 0)),
        ],
        out_specs=pl.BlockSpec((tb, vw), lambda b, s: (b * nt + fin(s)[0], fin(s)[1])),
        scratch_shapes=[
            pltpu.VMEM((n_v_heads, HEAD_DIM, HEAD_DIM), F32),
            pltpu.VMEM((n_v, tb, HEAD_DIM), BF16),
            pltpu.VMEM((n_v, tb, HEAD_DIM), F32),
            pltpu.VMEM((n_v, n_chunks, HEAD_DIM, HEAD_DIM), BF16),
            pltpu.VMEM((n_v, n_chunks, HEAD_DIM, HEAD_DIM), F32),
            pltpu.VMEM((n_v, n_chunks, SUBLANES, HEAD_DIM), F32),
        ],
        compiler_params=_params(("parallel", "arbitrary")),
        name="gated_delta",
    )(proj, proj, proj, proj, bd, bd_t, norm_w.reshape(1, HEAD_DIM))


def _rope_tables(seq):
    inv = 1.0 / (ROPE_THETA ** (jnp.arange(0, HEAD_DIM, 2, dtype=F32) / HEAD_DIM))
    ang = jnp.arange(seq, dtype=F32)[:, None] * inv[None, :]
    ang = jnp.concatenate([ang, ang], axis=-1)
    sign = jnp.where(jnp.arange(HEAD_DIM) < HEAD_DIM // 2, -1.0, 1.0).astype(F32)
    return jnp.cos(ang), jnp.sin(ang) * sign[None, :]


def _diff_attention_mixer(h, xn, w_qkv, layer, lq1, lk1, lq2, lk2, subln, w_o, lambda_init,
                          batch, seq):
    d = h.shape[1]
    cos, sin = _rope_tables(seq)
    qkv = _proj_rope(xn, w_qkv, layer, cos, sin, seq, d, 2 * d, HEAD_DIM ** -0.5)
    lam_vecs = jnp.stack([lq1, lk1, lq2, lk2]).astype(F32)
    attn = _diff_attention(qkv, lam_vecs, subln, batch, seq, lambda_init)
    return _out_proj(attn, w_o, layer, h)


def _gated_deltanet_mixer(h, xn, w_in, conv_w, layer, a_log, dt_bias, gdn_norm, w_o, batch, seq):
    n_v_heads = a_log.shape[0]
    val_dim = n_v_heads * HEAD_DIM
    conv_dim = conv_w.shape[2]
    key_dim = (conv_dim - val_dim) // 2
    n_k_heads = key_dim // HEAD_DIM
    main = conv_dim + val_dim
    proj = _proj_conv(xn, w_in.astype(BF16), conv_w, layer, seq, main)
    bd, bd_t = _beta_decay(xn, w_in[layer, :, main:main + n_v_heads],
                           w_in[layer, :, main + n_v_heads:], a_log, dt_bias)
    o = _gated_delta(proj, bd, bd_t, gdn_norm, batch, seq, n_k_heads, n_v_heads)
    return _out_proj(o, w_o, layer, h)


def kernel(x, ffn1_norm, ffn1_w_gate, ffn1_w_up, ffn1_w_down, mix_norm, ffn2_norm, ffn2_w_gate, ffn2_w_up, ffn2_w_down, da_w_qkv, da_lambda_q1, da_lambda_k1, da_lambda_q2, da_lambda_k2, da_subln, da_w_o, gdn_w_in, gdn_conv_w, gdn_a_log, gdn_dt_bias, gdn_norm, gdn_w_o, final_norm):
    batch, seq, d = x.shape
    depth = ffn1_norm.shape[0]
    h = x.reshape(batch * seq, d)
    for i in range(depth):
        h, xn = _ffn(h, ffn1_norm[i], ffn1_w_gate, ffn1_w_up, ffn1_w_down, i, "next", mix_norm[i])
        j = i // 2
        if i % 2 == 0:
            lambda_init = 0.8 - 0.6 * math.exp(-0.3 * i)
            h = _diff_attention_mixer(h, xn, da_w_qkv, j, da_lambda_q1[j], da_lambda_k1[j],
                                      da_lambda_q2[j], da_lambda_k2[j], da_subln[j], da_w_o,
                                      lambda_init, batch, seq)
        else:
            h = _gated_deltanet_mixer(h, xn, gdn_w_in, gdn_conv_w, j, gdn_a_log[j],
                                      gdn_dt_bias[j], gdn_norm[j], gdn_w_o, batch, seq)
        if i == depth - 1:
            h = _ffn(h, ffn2_norm[i], ffn2_w_gate, ffn2_w_up, ffn2_w_down, i, "final", final_norm)
        else:
            h = _ffn(h, ffn2_norm[i], ffn2_w_gate, ffn2_w_up, ffn2_w_down, i)
    return h.reshape(batch, seq, d)
```

```python
import functools
import math

import jax
import jax.numpy as jnp
from jax import lax
from jax.experimental import pallas as pl
from jax.experimental.pallas import tpu as pltpu

F32 = jnp.float32
BF16 = jnp.bfloat16

RMS_EPS = 1e-6
L2_EPS = 1e-6
ROPE_THETA = 10000.0
HEAD_DIM = 128
DA_V_DIM = 2 * HEAD_DIM
CHUNK = 64
CONV_WIDTH = 4
CONV_HALO = 16
SUBLANES = 8
MXU_COLS = 256
N_BETA_DECAY_LANES = 128
VMEM_LIMIT = 60 * 1024 * 1024
MASK_VALUE = -1e30


def _params(semantics):
    return pltpu.CompilerParams(dimension_semantics=semantics, vmem_limit_bytes=VMEM_LIMIT)


def _tile(total, preferred):
    tile = min(preferred, total)
    while total % tile:
        tile //= 2
    return tile


def _rms_normalize(x, w):
    ms = jnp.mean(x * x, axis=-1, keepdims=True)
    return x * lax.rsqrt(ms + RMS_EPS) * w


def _silu(x):
    return x * jax.nn.sigmoid(x)


def _dot(a, b):
    return jnp.dot(a, b, preferred_element_type=F32)


def _dot_nt(a, b):
    return lax.dot_general(a, b, (((1,), (1,)), ((), ())), preferred_element_type=F32)


def _dot_tn(a, b):
    return lax.dot_general(a, b, (((0,), (0,)), ((), ())), preferred_element_type=F32)


def _dot_exact(a, b):
    return jnp.dot(a, b, preferred_element_type=F32, precision=lax.Precision.HIGHEST)


def _ffn_body(h_ref, nw_ref, wg_ref, wu_ref, wd_ref, ew_ref, o_ref, xn_ref, *, epilogue):
    j = pl.program_id(1)

    @pl.when(j == 0)
    def _():
        xn_ref[...] = _rms_normalize(h_ref[...], nw_ref[...]).astype(BF16)
        o_ref[...] = jnp.zeros_like(o_ref)

    xn = xn_ref[...]
    g = _dot(xn, wg_ref[...].astype(BF16))
    u = _dot(xn, wu_ref[...].astype(BF16))
    a = (_silu(g) * u).astype(BF16)
    o_ref[...] += _dot(a, wd_ref[...].astype(BF16))

    @pl.when(j == pl.num_programs(1) - 1)
    def _():
        y = h_ref[...] + 0.5 * o_ref[...]
        if epilogue == "final":
            o_ref[...] = _rms_normalize(y, ew_ref[...])
        else:
            o_ref[...] = y
            if epilogue == "next":
                xn_ref[...] = _rms_normalize(y, ew_ref[...]).astype(BF16)


def _ffn(h, norm_w, w_gate, w_up, w_down, layer, epilogue=None, epilogue_w=None, *,
         tm=1024, tf=256):
    n, d = h.shape
    f = w_gate.shape[2]
    tm, tf = _tile(n, tm), _tile(f, tf)
    ew = norm_w if epilogue_w is None else epilogue_w
    row_block = pl.BlockSpec((tm, d), lambda i, j: (i, 0))
    out_shape = jax.ShapeDtypeStruct((n, d), F32)
    out_specs = row_block
    scratch_shapes = [pltpu.VMEM((tm, d), BF16)]
    h_block = row_block
    if epilogue == "next":
        out_shape = (out_shape, jax.ShapeDtypeStruct((n, d), BF16))
        out_specs = (row_block, row_block)
        scratch_shapes = []
        h_block = pl.BlockSpec((tm, d), lambda i, j: (i, 0), pipeline_mode=pl.Buffered(1))
    return pl.pallas_call(
        functools.partial(_ffn_body, epilogue=epilogue),
        out_shape=out_shape,
        grid=(n // tm, f // tf),
        in_specs=[
            h_block,
            pl.BlockSpec((1, d), lambda i, j: (0, 0)),
            pl.BlockSpec((None, d, tf), lambda i, j: (layer, 0, j)),
            pl.BlockSpec((None, d, tf), lambda i, j: (layer, 0, j)),
            pl.BlockSpec((None, tf, d), lambda i, j: (layer, j, 0)),
            pl.BlockSpec((1, d), lambda i, j: (0, 0)),
        ],
        out_specs=out_specs,
        scratch_shapes=scratch_shapes,
        compiler_params=_params(("parallel", "arbitrary")),
        name="ffn",
    )(h, norm_w.reshape(1, d), w_gate, w_up, w_down, ew.reshape(1, d))


def _proj_rope_body(x_ref, w_ref, cos_ref, sin_ref, o_ref, wb_ref, *,
                    n_q_tiles, n_rope_tiles, q_scale):
    j = pl.program_id(0)

    @pl.when(pl.program_id(1) == 0)
    def _():
        wb_ref[...] = w_ref[...].astype(BF16)

    tn = o_ref.shape[1]

    @pl.when(j < n_rope_tiles)
    def _():
        cos = cos_ref[...]
        sin = sin_ref[...]
        scale = jnp.where(j < n_q_tiles, q_scale, 1.0).astype(F32)
        for g in range(tn // MXU_COLS):
            y = _dot(x_ref[...], wb_ref[:, g * MXU_COLS:(g + 1) * MXU_COLS])
            for c in range(MXU_COLS // HEAD_DIM):
                x = y[:, c * HEAD_DIM:(c + 1) * HEAD_DIM]
                swapped = pltpu.roll(x, HEAD_DIM // 2, 1)
                lo = g * MXU_COLS + c * HEAD_DIM
                o_ref[:, lo:lo + HEAD_DIM] = (
                    (x * cos + swapped * sin) * scale).astype(o_ref.dtype)

    @pl.when(j >= n_rope_tiles)
    def _():
        o_ref[...] = _dot(x_ref[...], wb_ref[...]).astype(o_ref.dtype)


def _proj_rope(x, w, layer, cos, sin, seq, n_q_cols, n_rope_cols, q_scale, *, tm=1024, tn=1024):
    n, d = x.shape
    n_out = w.shape[2]
    tm, tn = _tile(seq, tm), _tile(n_q_cols, tn)
    assert n % tm == 0 and n_out % tn == 0 and n_rope_cols % tn == 0
    tiles_per_seq = seq // tm
    return pl.pallas_call(
        functools.partial(_proj_rope_body, n_q_tiles=n_q_cols // tn,
                          n_rope_tiles=n_rope_cols // tn, q_scale=q_scale),
        out_shape=jax.ShapeDtypeStruct((n, n_out), BF16),
        grid=(n_out // tn, n // tm),
        in_specs=[
            pl.BlockSpec((tm, d), lambda j, i: (i, 0)),
            pl.BlockSpec((None, d, tn), lambda j, i: (layer, 0, j)),
            pl.BlockSpec((tm, HEAD_DIM), lambda j, i: (i % tiles_per_seq, 0)),
            pl.BlockSpec((tm, HEAD_DIM), lambda j, i: (i % tiles_per_seq, 0)),
        ],
        out_specs=pl.BlockSpec((tm, tn), lambda j, i: (i, j)),
        scratch_shapes=[pltpu.VMEM((d, tn), BF16)],
        compiler_params=_params(("parallel", "arbitrary")),
        name="proj_rope",
    )(x, w, cos, sin)


def _proj_conv_body(x_ref, xh_ref, w_ref, cw_ref, o_ref, *, n_conv_tiles, tiles_per_seq):
    j = pl.program_id(0)
    i = pl.program_id(1)
    tm = x_ref.shape[0]
    keep = jnp.where(i % tiles_per_seq == 0, 0.0, 1.0).astype(BF16)

    @pl.when(j < n_conv_tiles)
    def _():
        lhs = jnp.concatenate([xh_ref[...] * keep, x_ref[...]], axis=0)
        for g in range(o_ref.shape[1] // MXU_COLS):
            cols = slice(g * MXU_COLS, (g + 1) * MXU_COLS)
            y = _dot(lhs, w_ref[:, cols])
            cw = cw_ref[:, cols]
            groups = y.shape[0] // SUBLANES
            halo_groups = CONV_HALO // SUBLANES
            y3 = y.reshape(groups, SUBLANES, MXU_COLS)
            sub = lax.broadcasted_iota(jnp.int32, (groups - halo_groups, SUBLANES, MXU_COLS), 1)
            acc = cw[CONV_WIDTH - 1:CONV_WIDTH, :][None] * y3[halo_groups:]
            rot = y3
            for back in range(1, CONV_WIDTH):
                tap = CONV_WIDTH - 1 - back
                rot = pltpu.roll(rot, 1, 1)
                shifted = jnp.where(sub >= back, rot[halo_groups:], rot[halo_groups - 1:-1])
                acc = acc + cw[tap:tap + 1, :][None] * shifted
            o_ref[:, cols] = _silu(acc).reshape(tm, MXU_COLS).astype(o_ref.dtype)

    @pl.when(j >= n_conv_tiles)
    def _():
        o_ref[...] = _dot(x_ref[...], w_ref[...]).astype(o_ref.dtype)


def _proj_conv(x, w, conv_w, layer, seq, n_out, *, tm=1024, tn=2048):
    assert w.dtype == BF16
    n, d = x.shape
    n_conv = conv_w.shape[2]
    tm, tn = _tile(seq, tm), _tile(math.gcd(n_conv, n_out), tn)
    assert n % tm == 0 and tm % CONV_HALO == 0
    assert CONV_WIDTH - 1 < SUBLANES <= CONV_HALO and CONV_HALO % SUBLANES == 0
    tiles_per_seq = seq // tm
    n_conv_tiles = n_conv // tn
    halo_blocks = tm // CONV_HALO
    return pl.pallas_call(
        functools.partial(_proj_conv_body, n_conv_tiles=n_conv_tiles,
                          tiles_per_seq=tiles_per_seq),
        out_shape=jax.ShapeDtypeStruct((n, n_out), BF16),
        grid=(n_out // tn, n // tm),
        in_specs=[
            pl.BlockSpec((tm, d), lambda j, i: (i, 0)),
            pl.BlockSpec((CONV_HALO, d), lambda j, i: (jnp.maximum(i * halo_blocks - 1, 0), 0)),
            pl.BlockSpec((None, d, tn), lambda j, i: (layer, 0, j)),
            pl.BlockSpec((None, CONV_WIDTH, tn),
                         lambda j, i: (layer, 0, jnp.minimum(j, n_conv_tiles - 1))),
        ],
        out_specs=pl.BlockSpec((tm, tn), lambda j, i: (i, j)),
        compiler_params=_params(("parallel", "parallel")),
        name="proj_conv",
    )(x, x, w, conv_w)


def _out_proj_body(x_ref, w_ref, r_ref, o_ref, wb_ref):
    @pl.when(pl.program_id(1) == 0)
    def _():
        wb_ref[...] = w_ref[...].astype(BF16)

    o_ref[...] = r_ref[...] + _dot(x_ref[...], wb_ref[...])


def _out_proj(x, w, layer, res, *, tm=1024, tn=512):
    n, k = x.shape
    d = w.shape[2]
    tm, tn = _tile(n, tm), _tile(d, tn)
    return pl.pallas_call(
        _out_proj_body,
        out_shape=jax.ShapeDtypeStruct((n, d), F32),
        grid=(d // tn, n // tm),
        in_specs=[
            pl.BlockSpec((tm, k), lambda j, i: (i, 0)),
            pl.BlockSpec((None, k, tn), lambda j, i: (layer, 0, j)),
            pl.BlockSpec((tm, tn), lambda j, i: (i, j)),
        ],
        out_specs=pl.BlockSpec((tm, tn), lambda j, i: (i, j)),
        scratch_shapes=[pltpu.VMEM((k, tn), BF16)],
        compiler_params=_params(("parallel", "arbitrary")),
        name="out_proj",
    )(x, w, res)


def _diff_attn_body(lam_ref, subln_ref, q_ref, k_ref, v_ref, o_ref, *, tq, nq, lambda_init):
    i = pl.program_id(2)
    lv = lam_ref[...]
    lam = (jnp.exp(jnp.sum(lv[0:1] * lv[1:2], axis=-1, keepdims=True))
           - jnp.exp(jnp.sum(lv[2:3] * lv[3:4], axis=-1, keepdims=True)) + lambda_init)
    key_pos = lax.broadcasted_iota(jnp.int32, (tq, tq), 0)
    query_pos = lax.broadcasted_iota(jnp.int32, (tq, tq), 1)
    causal = key_pos <= query_pos

    def attend_both(n_full):
        diag = slice(n_full, n_full + tq)
        cols = [slice(sub * HEAD_DIM, (sub + 1) * HEAD_DIM) for sub in (0, 1)]
        qs = [q_ref[:, c] for c in cols]
        s_diag = [jnp.where(causal, _dot_nt(k_ref[diag, c], q), MASK_VALUE)
                  for c, q in zip(cols, qs)]
        s_full = [_dot_nt(k_ref[0:n_full, c], q) for c, q in zip(cols, qs)] if n_full else None
        outs = []
        for sub in (0, 1):
            m = jnp.max(s_diag[sub], axis=0, keepdims=True)
            if n_full:
                m = jnp.maximum(m, jnp.max(s_full[sub], axis=0, keepdims=True))
            p_diag = jnp.exp(s_diag[sub] - m)
            l = jnp.sum(p_diag, axis=0, keepdims=True)
            acc = _dot_tn(v_ref[diag, :], p_diag.astype(BF16))
            if n_full:
                p_full = jnp.exp(s_full[sub] - m)
                l = l + jnp.sum(p_full, axis=0, keepdims=True)
                acc = acc + _dot_tn(v_ref[0:n_full, :], p_full.astype(BF16))
            outs.append(acc / l)
        return outs

    for c in range(nq):
        @pl.when(i == c)
        def _(c=c):
            o0, o1 = attend_both(c * tq)
            o = o0 - lam * o1
            ms = jnp.mean(o * o, axis=0, keepdims=True)
            y = o * lax.rsqrt(ms + RMS_EPS) * subln_ref[...] * (1.0 - lambda_init)
            o_ref[...] = y.T.astype(o_ref.dtype)


def _diff_attention(qkv, lam_vecs, subln, batch, seq, lambda_init, *, tq=512):
    n, width = qkv.shape
    d = width // 3
    heads = d // DA_V_DIM
    tq = _tile(seq, tq)
    nq = seq // tq
    return pl.pallas_call(
        functools.partial(_diff_attn_body, tq=tq, nq=nq, lambda_init=lambda_init),
        out_shape=jax.ShapeDtypeStruct((n, d), BF16),
        grid=(batch, heads, nq),
        in_specs=[
            pl.BlockSpec((4, HEAD_DIM), lambda b, h, i: (0, 0)),
            pl.BlockSpec((DA_V_DIM, 1), lambda b, h, i: (0, 0)),
            pl.BlockSpec((tq, DA_V_DIM), lambda b, h, i: (b * nq + i, h)),
            pl.BlockSpec((seq, DA_V_DIM), lambda b, h, i: (b, heads + h)),
            pl.BlockSpec((seq, DA_V_DIM), lambda b, h, i: (b, 2 * heads + h)),
        ],
        out_specs=pl.BlockSpec((tq, DA_V_DIM), lambda b, h, i: (b * nq + i, h)),
        compiler_params=_params(("parallel", "parallel", "arbitrary")),
        name="diff_attn",
    )(lam_vecs, subln.reshape(DA_V_DIM, 1), qkv, qkv, qkv)


def _beta_decay_body(x_ref, w_ref, alog_ref, dt_ref, o_ref, ot_ref, *, n_heads):
    tm = x_ref.shape[0]
    y = _dot(x_ref[...], w_ref[...])
    a = y + dt_ref[...]
    softplus = jnp.maximum(a, 0.0) + jnp.log1p(jnp.exp(-jnp.abs(a)))
    g = -jnp.exp(alog_ref[...]) * softplus
    row = lax.broadcasted_iota(jnp.int32, (tm, tm), 0)
    col = lax.broadcasted_iota(jnp.int32, (tm, tm), 1)
    same_chunk = (row // CHUNK) == (col // CHUNK)
    prefix = jnp.where(same_chunk & (col <= row), 1.0, 0.0).astype(F32)
    whole = jnp.where(same_chunk, 1.0, 0.0).astype(F32)
    g_cum = _dot_exact(prefix, g)
    g_all = _dot_exact(whole, g)
    lane = lax.broadcasted_iota(jnp.int32, y.shape, 1)
    packed = jnp.where(lane < n_heads, jax.nn.sigmoid(y),
                       jnp.where(lane < 2 * n_heads, g_cum,
                                 jnp.where(lane < 3 * n_heads, g_all, 0.0)))
    o_ref[...] = packed
    ot_ref[...] = packed.T


def _beta_decay(x, w_b, w_a, a_log, dt_bias, *, tm=256):
    n, d = x.shape
    n_heads = a_log.shape[0]
    lanes = N_BETA_DECAY_LANES
    assert n % tm == 0 and tm % CHUNK == 0 and 3 * n_heads <= lanes
    pad = lanes - 3 * n_heads
    w = jnp.concatenate([w_b, w_a, w_a, jnp.zeros((d, pad), w_a.dtype)], axis=1).astype(BF16)
    zeros = jnp.zeros((n_heads,), F32)
    a_log_l = jnp.concatenate([zeros, a_log, a_log, jnp.zeros((pad,), F32)]).reshape(1, lanes)
    dt_l = jnp.concatenate([zeros, dt_bias, dt_bias, jnp.zeros((pad,), F32)]).reshape(1, lanes)
    return pl.pallas_call(
        functools.partial(_beta_decay_body, n_heads=n_heads),
        out_shape=(jax.ShapeDtypeStruct((n, lanes), F32), jax.ShapeDtypeStruct((lanes, n), F32)),
        grid=(n // tm,),
        in_specs=[
            pl.BlockSpec((tm, d), lambda i: (i, 0)),
            pl.BlockSpec((d, lanes), lambda i: (0, 0)),
            pl.BlockSpec((1, lanes), lambda i: (0, 0)),
            pl.BlockSpec((1, lanes), lambda i: (0, 0)),
        ],
        out_specs=(pl.BlockSpec((tm, lanes), lambda i: (i, 0)),
                   pl.BlockSpec((lanes, tm), lambda i: (0, i))),
        compiler_params=_params(("parallel",)),
        name="beta_decay",
    )(x, w, a_log_l, dt_l)


def _delta_body(q_ref, k_ref, v_ref, z_ref, bd_ref, bdt_ref, nw_ref, o_ref,
                state_ref, qeff_ref, oloc_ref, m_ref, n_ref, g_ref, *,
                tb, n_heads, rep, khs, n_groups):
    s = pl.program_id(1)
    n_pairs = pl.num_programs(1) - 1
    n_v = khs * rep
    n_chunks = tb // CHUNK
    lanes = N_BETA_DECAY_LANES
    prev_group = jnp.maximum(s - 1, 0) % n_groups
    hv0 = (jnp.minimum(s, n_pairs - 1) % n_groups) * n_v

    @pl.when(s == 0)
    def _():
        state_ref[...] = jnp.zeros_like(state_ref)
        qeff_ref[...] = jnp.zeros_like(qeff_ref)
        oloc_ref[...] = jnp.zeros_like(oloc_ref)
        m_ref[...] = jnp.zeros_like(m_ref)
        n_ref[...] = jnp.zeros_like(n_ref)
        g_ref[...] = jnp.zeros_like(g_ref)

    def head_slice(x, i):
        return x[:, i * HEAD_DIM:(i + 1) * HEAD_DIM]

    states = [state_ref[prev_group * n_v + idx] for idx in range(n_v)]
    outs = [[] for _ in range(n_v)]
    for c in range(n_chunks):
        rows = slice(c * CHUNK, (c + 1) * CHUNK)
        for idx in range(n_v):
            lhs = jnp.concatenate([m_ref[idx, c], qeff_ref[idx, rows, :]], axis=0)
            ms = _dot(lhs, states[idx].astype(BF16))
            outs[idx].append(ms[HEAD_DIM:] + oloc_ref[idx, rows, :])
            states[idx] = (states[idx] * g_ref[idx, c][0:1, :] - ms[:HEAD_DIM] + n_ref[idx, c])
    nw = nw_ref[...]
    for idx in range(n_v):
        state_ref[prev_group * n_v + idx] = states[idx]
        o = jnp.concatenate(outs[idx], axis=0)
        z = head_slice(z_ref, idx).astype(F32)
        o = o * lax.rsqrt(jnp.mean(o * o, axis=-1, keepdims=True) + RMS_EPS) * nw * _silu(z)
        o_ref[:, idx * HEAD_DIM:(idx + 1) * HEAD_DIM] = o.astype(o_ref.dtype)

    qs, ks, kks, qks = [], [], [], []
    for a in range(khs):
        q = head_slice(q_ref, a).astype(F32)
        k = head_slice(k_ref, a).astype(F32)
        q = q * lax.rsqrt(jnp.sum(q * q, axis=-1, keepdims=True) + L2_EPS) * (HEAD_DIM ** -0.5)
        k = k * lax.rsqrt(jnp.sum(k * k, axis=-1, keepdims=True) + L2_EPS)
        qs.append(q)
        ks.append(k)
        k_b = k.astype(BF16)
        kks.append(_dot_nt(k_b, k_b))
        qks.append(_dot_nt(q.astype(BF16), k_b))

    row = lax.broadcasted_iota(jnp.int32, (tb, tb), 0)
    col = lax.broadcasted_iota(jnp.int32, (tb, tb), 1)
    same_chunk = (row // CHUNK) == (col // CHUNK)
    tril = same_chunk & (col <= row)
    strict = same_chunk & (col < row)

    bd = pltpu.roll(bd_ref[...], (lanes - hv0) % lanes, 1)

    p_cur, rhs, qk_d, q_g, k_g, g_chunk = [], [], [], [], [], []
    for idx in range(n_v):
        a = idx // rep
        beta_c = bd[:, idx:idx + 1]
        gcum_c = bd[:, n_heads + idx:n_heads + idx + 1]
        gall_c = bd[:, 2 * n_heads + idx:2 * n_heads + idx + 1]
        gcum_r = bdt_ref[pl.ds(n_heads + hv0 + idx, 1), :]
        e = jnp.exp(gcum_c - gcum_r)
        p_cur.append((kks[a] * jnp.where(strict, e, 0.0) * (-beta_c)).astype(BF16))
        qk_d.append((qks[a] * jnp.where(tril, e, 0.0)).astype(BF16))
        exp_g = jnp.exp(gcum_c)
        rhs.append(jnp.concatenate([head_slice(v_ref, idx).astype(F32) * beta_c,
                                    ks[a] * (beta_c * exp_g)], axis=-1))
        q_g.append(qs[a] * exp_g)
        k_g.append((ks[a] * jnp.exp(gall_c - gcum_c)).astype(BF16))
        g_chunk.append(jnp.exp(gall_c))

    n_factors = CHUNK.bit_length() - 1
    blk = HEAD_DIM
    n_blk = tb // blk
    p_cur = [[p[h * blk:(h + 1) * blk, h * blk:(h + 1) * blk] for h in range(n_blk)]
             for p in p_cur]
    rhs = [[r[h * blk:(h + 1) * blk] for h in range(n_blk)] for r in rhs]
    for j in range(n_factors):
        for idx in range(n_v):
            for h in range(n_blk):
                p_b = p_cur[idx][h]
                r_b = rhs[idx][h].astype(BF16)
                if j < n_factors - 1:
                    res = _dot(p_b, jnp.concatenate([p_b, r_b], axis=-1))
                    p_cur[idx][h] = res[:, :blk].astype(BF16)
                    rhs[idx][h] = rhs[idx][h] + res[:, blk:]
                else:
                    rhs[idx][h] = rhs[idx][h] + _dot(p_b, r_b)
    rhs = [jnp.concatenate(r, axis=0) for r in rhs]

    for idx in range(n_v):
        uw_b = rhs[idx].astype(BF16)
        quw = _dot(qk_d[idx], uw_b)
        oloc_ref[idx] = quw[:, :HEAD_DIM]
        qeff_ref[idx] = (q_g[idx] - quw[:, HEAD_DIM:]).astype(BF16)
        for c in range(n_chunks):
            rows = slice(c * CHUNK, (c + 1) * CHUNK)
            kuw = _dot_tn(k_g[idx][rows], uw_b[rows])
            n_ref[idx, c] = kuw[:, :HEAD_DIM]
            m_ref[idx, c] = kuw[:, HEAD_DIM:].astype(BF16)
            g_ref[idx, c] = jnp.broadcast_to(g_chunk[idx][c * CHUNK:c * CHUNK + 1, :],
                                             (SUBLANES, HEAD_DIM))


def _gated_delta(proj, bd, bd_t, norm_w, batch, seq, n_k_heads, n_v_heads, *, tb=256, khs=4):
    n = proj.shape[0]
    rep = n_v_heads // n_k_heads
    khs = _tile(n_k_heads, khs)
    n_groups = n_k_heads // khs
    n_v = khs * rep
    kw = khs * HEAD_DIM
    vw = n_v * HEAD_DIM
    key_dim = n_k_heads * HEAD_DIM
    val_dim = n_v_heads * HEAD_DIM
    assert seq % tb == 0 and tb % HEAD_DIM == 0 and HEAD_DIM % CHUNK == 0
    nt = seq // tb
    n_chunks = tb // CHUNK
    n_pairs = nt * n_groups
    k_blk = key_dim // kw
    v_blk = 2 * key_dim // vw
    z_blk = (2 * key_dim + val_dim) // vw
    lanes = N_BETA_DECAY_LANES

    def prep(s):
        pair = jnp.minimum(s, n_pairs - 1)
        return pair // n_groups, pair % n_groups

    def fin(s):
        pair = jnp.maximum(s - 1, 0)
        return pair // n_groups, pair % n_groups

    return pl.pallas_call(
        functools.partial(_delta_body, tb=tb, n_heads=n_v_heads, rep=rep, khs=khs,
                          n_groups=n_groups),
        out_shape=jax.ShapeDtypeStruct((n, val_dim), BF16),
        grid=(batch, n_pairs + 1),
        in_specs=[
            pl.BlockSpec((tb, kw), lambda b, s: (b * nt + prep(s)[0], prep(s)[1])),
            pl.BlockSpec((tb, kw), lambda b, s: (b * nt + prep(s)[0], k_blk + prep(s)[1])),
            pl.BlockSpec((tb, vw), lambda b, s: (b * nt + prep(s)[0], v_blk + prep(s)[1])),
            pl.BlockSpec((tb, vw), lambda b, s: (b * nt + fin(s)[0], z_blk + fin(s)[1])),
            pl.BlockSpec((tb, lanes), lambda b, s: (b * nt + prep(s)[0], 0)),
            pl.BlockSpec((lanes, tb), lambda b, s: (0, b * nt + prep(s)[0])),
            pl.BlockSpec((1, HEAD_DIM), lambda b, s: (0, 0)),
        ],
        out_specs=pl.BlockSpec((tb, vw), lambda b, s: (b * nt + fin(s)[0], fin(s)[1])),
        scratch_shapes=[
            pltpu.VMEM((n_v_heads, HEAD_DIM, HEAD_DIM), F32),
            pltpu.VMEM((n_v, tb, HEAD_DIM), BF16),
            pltpu.VMEM((n_v, tb, HEAD_DIM), F32),
            pltpu.VMEM((n_v, n_chunks, HEAD_DIM, HEAD_DIM), BF16),
            pltpu.VMEM((n_v, n_chunks, HEAD_DIM, HEAD_DIM), F32),
            pltpu.VMEM((n_v, n_chunks, SUBLANES, HEAD_DIM), F32),
        ],
        compiler_params=_params(("parallel", "arbitrary")),
        name="gated_delta",
    )(proj, proj, proj, proj, bd, bd_t, norm_w.reshape(1, HEAD_DIM))


def _rope_tables(seq):
    inv = 1.0 / (ROPE_THETA ** (jnp.arange(0, HEAD_DIM, 2, dtype=F32) / HEAD_DIM))
    ang = jnp.arange(seq, dtype=F32)[:, None] * inv[None, :]
    ang = jnp.concatenate([ang, ang], axis=-1)
    sign = jnp.where(jnp.arange(HEAD_DIM) < HEAD_DIM // 2, -1.0, 1.0).astype(F32)
    return jnp.cos(ang), jnp.sin(ang) * sign[None, :]


def _diff_attention_mixer(h, xn, w_qkv, layer, lq1, lk1, lq2, lk2, subln, w_o, lambda_init,
                          batch, seq):
    d = h.shape[1]
    cos, sin = _rope_tables(seq)
    qkv = _proj_rope(xn, w_qkv, layer, cos, sin, seq, d, 2 * d, HEAD_DIM ** -0.5)
    lam_vecs = jnp.stack([lq1, lk1, lq2, lk2]).astype(F32)
    attn = _diff_attention(qkv, lam_vecs, subln, batch, seq, lambda_init)
    return _out_proj(attn, w_o, layer, h)


def _gated_deltanet_mixer(h, xn, w_in, conv_w, layer, a_log, dt_bias, gdn_norm, w_o, batch, seq):
    n_v_heads = a_log.shape[0]
    val_dim = n_v_heads * HEAD_DIM
    conv_dim = conv_w.shape[2]
    key_dim = (conv_dim - val_dim) // 2
    n_k_heads = key_dim // HEAD_DIM
    main = conv_dim + val_dim
    proj = _proj_conv(xn, w_in.astype(BF16), conv_w, layer, seq, main)
    bd, bd_t = _beta_decay(xn, w_in[layer, :, main:main + n_v_heads],
                           w_in[layer, :, main + n_v_heads:], a_log, dt_bias)
    o = _gated_delta(proj, bd, bd_t, gdn_norm, batch, seq, n_k_heads, n_v_heads)
    return _out_proj(o, w_o, layer, h)


def kernel(x, ffn1_norm, ffn1_w_gate, ffn1_w_up, ffn1_w_down, mix_norm, ffn2_norm, ffn2_w_gate, ffn2_w_up, ffn2_w_down, da_w_qkv, da_lambda_q1, da_lambda_k1, da_lambda_q2, da_lambda_k2, da_subln, da_w_o, gdn_w_in, gdn_conv_w, gdn_a_log, gdn_dt_bias, gdn_norm, gdn_w_o, final_norm):
    batch, seq, d = x.shape
    depth = ffn1_norm.shape[0]
    h = x.reshape(batch * seq, d)
    for i in range(depth):
        h, xn = _ffn(h, ffn1_norm[i], ffn1_w_gate, ffn1_w_up, ffn1_w_down, i, "next", mix_norm[i])
        j = i // 2
        if i % 2 == 0:
            lambda_init = 0.8 - 0.6 * math.exp(-0.3 * i)
            h = _diff_attention_mixer(h, xn, da_w_qkv, j, da_lambda_q1[j], da_lambda_k1[j],
                                      da_lambda_q2[j], da_lambda_k2[j], da_subln[j], da_w_o,
                                      lambda_init, batch, seq)
        else:
            h = _gated_deltanet_mixer(h, xn, gdn_w_in, gdn_conv_w, j, gdn_a_log[j],
                                      gdn_dt_bias[j], gdn_norm[j], gdn_w_o, batch, seq)
        if i == depth - 1:
            h = _ffn(h, ffn2_norm[i], ffn2_w_gate, ffn2_w_up, ffn2_w_down, i, "final", final_norm)
        else:
            h = _ffn(h, ffn2_norm[i], ffn2_w_gate, ffn2_w_up, ffn2_w_down, i)
    return h.reshape(batch, seq, d)
```

```python
import functools
import math

import jax
import jax.numpy as jnp
from jax import lax
from jax.experimental import pallas as pl
from jax.experimental.pallas import tpu as pltpu

F32 = jnp.float32
BF16 = jnp.bfloat16

RMS_EPS = 1e-6
L2_EPS = 1e-6
ROPE_THETA = 10000.0
HEAD_DIM = 128
DA_V_DIM = 2 * HEAD_DIM
CHUNK = 64
CONV_WIDTH = 4
CONV_HALO = 16
SUBLANES = 8
MXU_COLS = 256
N_BETA_DECAY_LANES = 128
VMEM_LIMIT = 56 * 1024 * 1024
FFN_VMEM_LIMIT = 62 * 1024 * 1024
MASK_VALUE = -1e30


def _params(semantics, vmem_limit=VMEM_LIMIT):
    return pltpu.CompilerParams(dimension_semantics=semantics, vmem_limit_bytes=vmem_limit)


def _tile(total, preferred):
    tile = min(preferred, total)
    while total % tile:
        tile //= 2
    return tile


def _rms_normalize(x, w):
    ms = jnp.mean(x * x, axis=-1, keepdims=True)
    return x * lax.rsqrt(ms + RMS_EPS) * w


def _silu(x):
    return x * jax.nn.sigmoid(x)


def _dot(a, b):
    return jnp.dot(a, b, preferred_element_type=F32)


def _dot_nt(a, b):
    return lax.dot_general(a, b, (((1,), (1,)), ((), ())), preferred_element_type=F32)


def _dot_tn(a, b):
    return lax.dot_general(a, b, (((0,), (0,)), ((), ())), preferred_element_type=F32)


def _dot_exact(a, b):
    return jnp.dot(a, b, preferred_element_type=F32, precision=lax.Precision.HIGHEST)


def _ffn_body(h_ref, nw_ref, wg_ref, wu_ref, wd_ref, ew_ref, o_ref, xn_ref, *, epilogue):
    j = pl.program_id(1)

    @pl.when(j == 0)
    def _():
        xn_ref[...] = _rms_normalize(h_ref[...], nw_ref[...]).astype(BF16)
        o_ref[...] = jnp.zeros_like(o_ref)

    xn = xn_ref[...]
    g = _dot(xn, wg_ref[...].astype(BF16))
    u = _dot(xn, wu_ref[...].astype(BF16))
    a = (_silu(g) * u).astype(BF16)
    o_ref[...] += _dot(a, wd_ref[...].astype(BF16))

    @pl.when(j == pl.num_programs(1) - 1)
    def _():
        y = h_ref[...] + 0.5 * o_ref[...]
        if epilogue == "final":
            o_ref[...] = _rms_normalize(y, ew_ref[...])
        else:
            o_ref[...] = y
            if epilogue == "next":
                xn_ref[...] = _rms_normalize(y, ew_ref[...]).astype(BF16)


def _ffn(h, norm_w, w_gate, w_up, w_down, layer, epilogue=None, epilogue_w=None, *,
         tm=1024, tf=256):
    n, d = h.shape
    f = w_gate.shape[2]
    tm, tf = _tile(n, tm), _tile(f, tf)
    ew = norm_w if epilogue_w is None else epilogue_w
    row_block = pl.BlockSpec((tm, d), lambda i, j: (i, 0))
    out_shape = jax.ShapeDtypeStruct((n, d), F32)
    out_specs = row_block
    scratch_shapes = [pltpu.VMEM((tm, d), BF16)]
    if epilogue == "next":
        out_shape = (out_shape, jax.ShapeDtypeStruct((n, d), BF16))
        out_specs = (row_block, row_block)
        scratch_shapes = []
    return pl.pallas_call(
        functools.partial(_ffn_body, epilogue=epilogue),
        out_shape=out_shape,
        grid=(n // tm, f // tf),
        in_specs=[
            row_block,
            pl.BlockSpec((1, d), lambda i, j: (0, 0)),
            pl.BlockSpec((None, d, tf), lambda i, j: (layer, 0, j)),
            pl.BlockSpec((None, d, tf), lambda i, j: (layer, 0, j)),
            pl.BlockSpec((None, tf, d), lambda i, j: (layer, j, 0)),
            pl.BlockSpec((1, d), lambda i, j: (0, 0)),
        ],
        out_specs=out_specs,
        scratch_shapes=scratch_shapes,
        compiler_params=_params(("parallel", "arbitrary"), FFN_VMEM_LIMIT),
        name="ffn",
    )(h, norm_w.reshape(1, d), w_gate, w_up, w_down, ew.reshape(1, d))


def _proj_rope_body(x_ref, w_ref, cos_ref, sin_ref, o_ref, wb_ref, *,
                    n_q_tiles, n_rope_tiles, q_scale):
    j = pl.program_id(0)

    @pl.when(pl.program_id(1) == 0)
    def _():
        wb_ref[...] = w_ref[...].astype(BF16)

    tn = o_ref.shape[1]

    @pl.when(j < n_rope_tiles)
    def _():
        cos = cos_ref[...]
        sin = sin_ref[...]
        scale = jnp.where(j < n_q_tiles, q_scale, 1.0).astype(F32)
        for g in range(tn // MXU_COLS):
            y = _dot(x_ref[...], wb_ref[:, g * MXU_COLS:(g + 1) * MXU_COLS])
            for c in range(MXU_COLS // HEAD_DIM):
                x = y[:, c * HEAD_DIM:(c + 1) * HEAD_DIM]
                swapped = pltpu.roll(x, HEAD_DIM // 2, 1)
                lo = g * MXU_COLS + c * HEAD_DIM
                o_ref[:, lo:lo + HEAD_DIM] = (
                    (x * cos + swapped * sin) * scale).astype(o_ref.dtype)

    @pl.when(j >= n_rope_tiles)
    def _():
        o_ref[...] = _dot(x_ref[...], wb_ref[...]).astype(o_ref.dtype)


def _proj_rope(x, w, layer, cos, sin, seq, n_q_cols, n_rope_cols, q_scale, *, tm=1024, tn=1024):
    n, d = x.shape
    n_out = w.shape[2]
    tm, tn = _tile(seq, tm), _tile(n_q_cols, tn)
    assert n % tm == 0 and n_out % tn == 0 and n_rope_cols % tn == 0
    tiles_per_seq = seq // tm
    return pl.pallas_call(
        functools.partial(_proj_rope_body, n_q_tiles=n_q_cols // tn,
                          n_rope_tiles=n_rope_cols // tn, q_scale=q_scale),
        out_shape=jax.ShapeDtypeStruct((n, n_out), BF16),
        grid=(n_out // tn, n // tm),
        in_specs=[
            pl.BlockSpec((tm, d), lambda j, i: (i, 0)),
            pl.BlockSpec((None, d, tn), lambda j, i: (layer, 0, j)),
            pl.BlockSpec((tm, HEAD_DIM), lambda j, i: (i % tiles_per_seq, 0)),
            pl.BlockSpec((tm, HEAD_DIM), lambda j, i: (i % tiles_per_seq, 0)),
        ],
        out_specs=pl.BlockSpec((tm, tn), lambda j, i: (i, j)),
        scratch_shapes=[pltpu.VMEM((d, tn), BF16)],
        compiler_params=_params(("parallel", "arbitrary")),
        name="proj_rope",
    )(x, w, cos, sin)


def _proj_conv_body(x_ref, xh_ref, w_ref, cw_ref, o_ref, *, n_conv_tiles, tiles_per_seq):
    j = pl.program_id(0)
    i = pl.program_id(1)
    tm = x_ref.shape[0]
    keep = jnp.where(i % tiles_per_seq == 0, 0.0, 1.0).astype(BF16)

    @pl.when(j < n_conv_tiles)
    def _():
        lhs = jnp.concatenate([xh_ref[...] * keep, x_ref[...]], axis=0)
        for g in range(o_ref.shape[1] // MXU_COLS):
            cols = slice(g * MXU_COLS, (g + 1) * MXU_COLS)
            y = _dot(lhs, w_ref[:, cols])
            cw = cw_ref[:, cols]
            groups = y.shape[0] // SUBLANES
            halo_groups = CONV_HALO // SUBLANES
            y3 = y.reshape(groups, SUBLANES, MXU_COLS)
            sub = lax.broadcasted_iota(jnp.int32, (groups - halo_groups, SUBLANES, MXU_COLS), 1)
            acc = cw[CONV_WIDTH - 1:CONV_WIDTH, :][None] * y3[halo_groups:]
            rot = y3
            for back in range(1, CONV_WIDTH):
                tap = CONV_WIDTH - 1 - back
                rot = pltpu.roll(rot, 1, 1)
                shifted = jnp.where(sub >= back, rot[halo_groups:], rot[halo_groups - 1:-1])
                acc = acc + cw[tap:tap + 1, :][None] * shifted
            o_ref[:, cols] = _silu(acc).reshape(tm, MXU_COLS).astype(o_ref.dtype)

    @pl.when(j >= n_conv_tiles)
    def _():
        o_ref[...] = _dot(x_ref[...], w_ref[...]).astype(o_ref.dtype)


def _proj_conv(x, w, conv_w, layer, seq, n_out, *, tm=1024, tn=2048):
    assert w.dtype == BF16
    n, d = x.shape
    n_conv = conv_w.shape[2]
    tm, tn = _tile(seq, tm), _tile(math.gcd(n_conv, n_out), tn)
    assert n % tm == 0 and tm % CONV_HALO == 0
    assert CONV_WIDTH - 1 < SUBLANES <= CONV_HALO and CONV_HALO % SUBLANES == 0
    tiles_per_seq = seq // tm
    n_conv_tiles = n_conv // tn
    halo_blocks = tm // CONV_HALO
    return pl.pallas_call(
        functools.partial(_proj_conv_body, n_conv_tiles=n_conv_tiles,
                          tiles_per_seq=tiles_per_seq),
        out_shape=jax.ShapeDtypeStruct((n, n_out), BF16),
        grid=(n_out // tn, n // tm),
        in_specs=[
            pl.BlockSpec((tm, d), lambda j, i: (i, 0)),
            pl.BlockSpec((CONV_HALO, d), lambda j, i: (jnp.maximum(i * halo_blocks - 1, 0), 0)),
            pl.BlockSpec((None, d, tn), lambda j, i: (layer, 0, j)),
            pl.BlockSpec((None, CONV_WIDTH, tn),
                         lambda j, i: (layer, 0, jnp.minimum(j, n_conv_tiles - 1))),
        ],
        out_specs=pl.BlockSpec((tm, tn), lambda j, i: (i, j)),
        compiler_params=_params(("parallel", "parallel")),
        name="proj_conv",
    )(x, x, w, conv_w)


def _out_proj_body(x_ref, w_ref, r_ref, o_ref, wb_ref):
    @pl.when(pl.program_id(1) == 0)
    def _():
        wb_ref[...] = w_ref[...].astype(BF16)

    o_ref[...] = r_ref[...] + _dot(x_ref[...], wb_ref[...])


def _out_proj(x, w, layer, res, *, tm=1024, tn=512):
    n, k = x.shape
    d = w.shape[2]
    tm, tn = _tile(n, tm), _tile(d, tn)
    return pl.pallas_call(
        _out_proj_body,
        out_shape=jax.ShapeDtypeStruct((n, d), F32),
        grid=(d // tn, n // tm),
        in_specs=[
            pl.BlockSpec((tm, k), lambda j, i: (i, 0)),
            pl.BlockSpec((None, k, tn), lambda j, i: (layer, 0, j)),
            pl.BlockSpec((tm, tn), lambda j, i: (i, j)),
        ],
        out_specs=pl.BlockSpec((tm, tn), lambda j, i: (i, j)),
        scratch_shapes=[pltpu.VMEM((k, tn), BF16)],
        compiler_params=_params(("parallel", "arbitrary")),
        name="out_proj",
    )(x, w, res)


def _diff_attn_body(lam_ref, subln_ref, q_ref, k_ref, v_ref, o_ref, *, tq, nq, lambda_init):
    i = pl.program_id(2)
    lv = lam_ref[...]
    lam = (jnp.exp(jnp.sum(lv[0:1] * lv[1:2], axis=-1, keepdims=True))
           - jnp.exp(jnp.sum(lv[2:3] * lv[3:4], axis=-1, keepdims=True)) + lambda_init)
    key_pos = lax.broadcasted_iota(jnp.int32, (tq, tq), 0)
    query_pos = lax.broadcasted_iota(jnp.int32, (tq, tq), 1)
    causal = key_pos <= query_pos

    def attend_both(n_full):
        diag = slice(n_full, n_full + tq)
        cols = [slice(sub * HEAD_DIM, (sub + 1) * HEAD_DIM) for sub in (0, 1)]
        qs = [q_ref[:, c] for c in cols]
        s_diag = [jnp.where(causal, _dot_nt(k_ref[diag, c], q), MASK_VALUE)
                  for c, q in zip(cols, qs)]
        s_full = [_dot_nt(k_ref[0:n_full, c], q) for c, q in zip(cols, qs)] if n_full else None
        outs = []
        for sub in (0, 1):
            m = jnp.max(s_diag[sub], axis=0, keepdims=True)
            if n_full:
                m = jnp.maximum(m, jnp.max(s_full[sub], axis=0, keepdims=True))
            p_diag = jnp.exp(s_diag[sub] - m)
            l = jnp.sum(p_diag, axis=0, keepdims=True)
            acc = _dot_tn(v_ref[diag, :], p_diag.astype(BF16))
            if n_full:
                p_full = jnp.exp(s_full[sub] - m)
                l = l + jnp.sum(p_full, axis=0, keepdims=True)
                acc = acc + _dot_tn(v_ref[0:n_full, :], p_full.astype(BF16))
            outs.append(acc / l)
        return outs

    for c in range(nq):
        @pl.when(i == c)
        def _(c=c):
            o0, o1 = attend_both(c * tq)
            o = o0 - lam * o1
            ms = jnp.mean(o * o, axis=0, keepdims=True)
            y = o * lax.rsqrt(ms + RMS_EPS) * subln_ref[...] * (1.0 - lambda_init)
            o_ref[...] = y.T.astype(o_ref.dtype)


def _diff_attention(qkv, lam_vecs, subln, batch, seq, lambda_init, *, tq=512):
    n, width = qkv.shape
    d = width // 3
    heads = d // DA_V_DIM
    tq = _tile(seq, tq)
    nq = seq // tq
    return pl.pallas_call(
        functools.partial(_diff_attn_body, tq=tq, nq=nq, lambda_init=lambda_init),
        out_shape=jax.ShapeDtypeStruct((n, d), BF16),
        grid=(batch, heads, nq),
        in_specs=[
            pl.BlockSpec((4, HEAD_DIM), lambda b, h, i: (0, 0)),
            pl.BlockSpec((DA_V_DIM, 1), lambda b, h, i: (0, 0)),
            pl.BlockSpec((tq, DA_V_DIM), lambda b, h, i: (b * nq + i, h)),
            pl.BlockSpec((seq, DA_V_DIM), lambda b, h, i: (b, heads + h)),
            pl.BlockSpec((seq, DA_V_DIM), lambda b, h, i: (b, 2 * heads + h)),
        ],
        out_specs=pl.BlockSpec((tq, DA_V_DIM), lambda b, h, i: (b * nq + i, h)),
        compiler_params=_params(("parallel", "parallel", "arbitrary")),
        name="diff_attn",
    )(lam_vecs, subln.reshape(DA_V_DIM, 1), qkv, qkv, qkv)


def _beta_decay_body(x_ref, w_ref, alog_ref, dt_ref, o_ref, ot_ref, *, n_heads):
    tm = x_ref.shape[0]
    y = _dot(x_ref[...], w_ref[...])
    a = y + dt_ref[...]
    softplus = jnp.maximum(a, 0.0) + jnp.log1p(jnp.exp(-jnp.abs(a)))
    g = -jnp.exp(alog_ref[...]) * softplus
    row = lax.broadcasted_iota(jnp.int32, (tm, tm), 0)
    col = lax.broadcasted_iota(jnp.int32, (tm, tm), 1)
    same_chunk = (row // CHUNK) == (col // CHUNK)
    prefix = jnp.where(same_chunk & (col <= row), 1.0, 0.0).astype(F32)
    whole = jnp.where(same_chunk, 1.0, 0.0).astype(F32)
    g_cum = _dot_exact(prefix, g)
    g_all = _dot_exact(whole, g)
    lane = lax.broadcasted_iota(jnp.int32, y.shape, 1)
    packed = jnp.where(lane < n_heads, jax.nn.sigmoid(y),
                       jnp.where(lane < 2 * n_heads, g_cum,
                                 jnp.where(lane < 3 * n_heads, g_all, 0.0)))
    o_ref[...] = packed
    ot_ref[...] = packed.T


def _beta_decay(x, w_b, w_a, a_log, dt_bias, *, tm=256):
    n, d = x.shape
    n_heads = a_log.shape[0]
    lanes = N_BETA_DECAY_LANES
    assert n % tm == 0 and tm % CHUNK == 0 and 3 * n_heads <= lanes
    pad = lanes - 3 * n_heads
    w = jnp.concatenate([w_b, w_a, w_a, jnp.zeros((d, pad), w_a.dtype)], axis=1).astype(BF16)
    zeros = jnp.zeros((n_heads,), F32)
    a_log_l = jnp.concatenate([zeros, a_log, a_log, jnp.zeros((pad,), F32)]).reshape(1, lanes)
    dt_l = jnp.concatenate([zeros, dt_bias, dt_bias, jnp.zeros((pad,), F32)]).reshape(1, lanes)
    return pl.pallas_call(
        functools.partial(_beta_decay_body, n_heads=n_heads),
        out_shape=(jax.ShapeDtypeStruct((n, lanes), F32), jax.ShapeDtypeStruct((lanes, n), F32)),
        grid=(n // tm,),
        in_specs=[
            pl.BlockSpec((tm, d), lambda i: (i, 0)),
            pl.BlockSpec((d, lanes), lambda i: (0, 0)),
            pl.BlockSpec((1, lanes), lambda i: (0, 0)),
            pl.BlockSpec((1, lanes), lambda i: (0, 0)),
        ],
        out_specs=(pl.BlockSpec((tm, lanes), lambda i: (i, 0)),
                   pl.BlockSpec((lanes, tm), lambda i: (0, i))),
        compiler_params=_params(("parallel",)),
        name="beta_decay",
    )(x, w, a_log_l, dt_l)


def _delta_body(q_ref, k_ref, v_ref, z_ref, bd_ref, bdt_ref, nw_ref, o_ref,
                state_ref, qeff_ref, oloc_ref, m_ref, n_ref, g_ref, *,
                tb, n_heads, rep, khs, n_groups):
    s = pl.program_id(1)
    n_pairs = pl.num_programs(1) - 1
    n_v = khs * rep
    n_chunks = tb // CHUNK
    lanes = N_BETA_DECAY_LANES
    prev_group = jnp.maximum(s - 1, 0) % n_groups
    hv0 = (jnp.minimum(s, n_pairs - 1) % n_groups) * n_v

    @pl.when(s == 0)
    def _():
        state_ref[...] = jnp.zeros_like(state_ref)
        qeff_ref[...] = jnp.zeros_like(qeff_ref)
        oloc_ref[...] = jnp.zeros_like(oloc_ref)
        m_ref[...] = jnp.zeros_like(m_ref)
        n_ref[...] = jnp.zeros_like(n_ref)
        g_ref[...] = jnp.zeros_like(g_ref)

    def head_slice(x, i):
        return x[:, i * HEAD_DIM:(i + 1) * HEAD_DIM]

    states = [state_ref[prev_group * n_v + idx] for idx in range(n_v)]
    outs = [[] for _ in range(n_v)]
    for c in range(n_chunks):
        rows = slice(c * CHUNK, (c + 1) * CHUNK)
        for idx in range(n_v):
            lhs = jnp.concatenate([m_ref[idx, c], qeff_ref[idx, rows, :]], axis=0)
            ms = _dot(lhs, states[idx].astype(BF16))
            outs[idx].append(ms[HEAD_DIM:] + oloc_ref[idx, rows, :])
            states[idx] = (states[idx] * g_ref[idx, c][0:1, :] - ms[:HEAD_DIM] + n_ref[idx, c])
    nw = nw_ref[...]
    for idx in range(n_v):
        state_ref[prev_group * n_v + idx] = states[idx]
        o = jnp.concatenate(outs[idx], axis=0)
        z = head_slice(z_ref, idx).astype(F32)
        o = o * lax.rsqrt(jnp.mean(o * o, axis=-1, keepdims=True) + RMS_EPS) * nw * _silu(z)
        o_ref[:, idx * HEAD_DIM:(idx + 1) * HEAD_DIM] = o.astype(o_ref.dtype)

    qs, ks, kks, qks = [], [], [], []
    for a in range(khs):
        q = head_slice(q_ref, a).astype(F32)
        k = head_slice(k_ref, a).astype(F32)
        q = q * lax.rsqrt(jnp.sum(q * q, axis=-1, keepdims=True) + L2_EPS) * (HEAD_DIM ** -0.5)
        k = k * lax.rsqrt(jnp.sum(k * k, axis=-1, keepdims=True) + L2_EPS)
        qs.append(q)
        ks.append(k)
        k_b = k.astype(BF16)
        kks.append(_dot_nt(k_b, k_b))
        qks.append(_dot_nt(q.astype(BF16), k_b))

    row = lax.broadcasted_iota(jnp.int32, (tb, tb), 0)
    col = lax.broadcasted_iota(jnp.int32, (tb, tb), 1)
    same_chunk = (row // CHUNK) == (col // CHUNK)
    tril = same_chunk & (col <= row)
    strict = same_chunk & (col < row)

    bd = pltpu.roll(bd_ref[...], (lanes - hv0) % lanes, 1)

    p_cur, rhs, qk_d, q_g, k_g, g_chunk = [], [], [], [], [], []
    for idx in range(n_v):
        a = idx // rep
        beta_c = bd[:, idx:idx + 1]
        gcum_c = bd[:, n_heads + idx:n_heads + idx + 1]
        gall_c = bd[:, 2 * n_heads + idx:2 * n_heads + idx + 1]
        gcum_r = bdt_ref[pl.ds(n_heads + hv0 + idx, 1), :]
        e = jnp.exp(gcum_c - gcum_r)
        p_cur.append((kks[a] * jnp.where(strict, e, 0.0) * (-beta_c)).astype(BF16))
        qk_d.append((qks[a] * jnp.where(tril, e, 0.0)).astype(BF16))
        exp_g = jnp.exp(gcum_c)
        rhs.append(jnp.concatenate([head_slice(v_ref, idx).astype(F32) * beta_c,
                                    ks[a] * (beta_c * exp_g)], axis=-1))
        q_g.append(qs[a] * exp_g)
        k_g.append((ks[a] * jnp.exp(gall_c - gcum_c)).astype(BF16))
        g_chunk.append(jnp.exp(gall_c))

    n_factors = CHUNK.bit_length() - 1
    blk = HEAD_DIM
    n_blk = tb // blk
    p_cur = [[p[h * blk:(h + 1) * blk, h * blk:(h + 1) * blk] for h in range(n_blk)]
             for p in p_cur]
    rhs = [[r[h * blk:(h + 1) * blk] for h in range(n_blk)] for r in rhs]
    for j in range(n_factors):
        for idx in range(n_v):
            for h in range(n_blk):
                p_b = p_cur[idx][h]
                r_b = rhs[idx][h].astype(BF16)
                if j < n_factors - 1:
                    res = _dot(p_b, jnp.concatenate([p_b, r_b], axis=-1))
                    p_cur[idx][h] = res[:, :blk].astype(BF16)
                    rhs[idx][h] = rhs[idx][h] + res[:, blk:]
                else:
                    rhs[idx][h] = rhs[idx][h] + _dot(p_b, r_b)
    rhs = [jnp.concatenate(r, axis=0) for r in rhs]

    for idx in range(n_v):
        uw_b = rhs[idx].astype(BF16)
        quw = _dot(qk_d[idx], uw_b)
        oloc_ref[idx] = quw[:, :HEAD_DIM]
        qeff_ref[idx] = (q_g[idx] - quw[:, HEAD_DIM:]).astype(BF16)
        for c in range(n_chunks):
            rows = slice(c * CHUNK, (c + 1) * CHUNK)
            kuw = _dot_tn(k_g[idx][rows], uw_b[rows])
            n_ref[idx, c] = kuw[:, :HEAD_DIM]
            m_ref[idx, c] = kuw[:, HEAD_DIM:].astype(BF16)
            g_ref[idx, c] = jnp.broadcast_to(g_chunk[idx][c * CHUNK:c * CHUNK + 1, :],
                                             (SUBLANES, HEAD_DIM))


def _gated_delta(proj, bd, bd_t, norm_w, batch, seq, n_k_heads, n_v_heads, *, tb=256, khs=4):
    n = proj.shape[0]
    rep = n_v_heads // n_k_heads
    khs = _tile(n_k_heads, khs)
    n_groups = n_k_heads // khs
    n_v = khs * rep
    kw = khs * HEAD_DIM
    vw = n_v * HEAD_DIM
    key_dim = n_k_heads * HEAD_DIM
    val_dim = n_v_heads * HEAD_DIM
    assert seq % tb == 0 and tb % HEAD_DIM == 0 and HEAD_DIM % CHUNK == 0
    nt = seq // tb
    n_chunks = tb // CHUNK
    n_pairs = nt * n_groups
    k_blk = key_dim // kw
    v_blk = 2 * key_dim // vw
    z_blk = (2 * key_dim + val_dim) // vw
    lanes = N_BETA_DECAY_LANES

    def prep(s):
        pair = jnp.minimum(s, n_pairs - 1)
        return pair // n_groups, pair % n_groups

    def fin(s):
        pair = jnp.maximum(s - 1, 0)
        return pair // n_groups, pair % n_groups

    return pl.pallas_call(
        functools.partial(_delta_body, tb=tb, n_heads=n_v_heads, rep=rep, khs=khs,
                          n_groups=n_groups),
        out_shape=jax.ShapeDtypeStruct((n, val_dim), BF16),
        grid=(batch, n_pairs + 1),
        in_specs=[
            pl.BlockSpec((tb, kw), lambda b, s: (b * nt + prep(s)[0], prep(s)[1])),
            pl.BlockSpec((tb, kw), lambda b, s: (b * nt + prep(s)[0], k_blk + prep(s)[1])),
            pl.BlockSpec((tb, vw), lambda b, s: (b * nt + prep(s)[0], v_blk + prep(s)[1])),
            pl.BlockSpec((tb, vw), lambda b, s: (b * nt + fin(s)[0], z_blk + fin(s)[1])),
            pl.BlockSpec((tb, lanes), lambda b, s: (b * nt + prep(s)[0], 0)),
            pl.BlockSpec((lanes, tb), lambda b, s: (0, b * nt + prep(s)[0])),
            pl.BlockSpec((1, HEAD_DIM), lambda b, s: (0, 0)),
        ],
        out_specs=pl.BlockSpec((tb, vw), lambda b, s: (b * nt + fin(s)[0], fin(s)[1])),
        scratch_shapes=[
            pltpu.VMEM((n_v_heads, HEAD_DIM, HEAD_DIM), F32),
            pltpu.VMEM((n_v, tb, HEAD_DIM), BF16),
            pltpu.VMEM((n_v, tb, HEAD_DIM), F32),
            pltpu.VMEM((n_v, n_chunks, HEAD_DIM, HEAD_DIM), BF16),
            pltpu.VMEM((n_v, n_chunks, HEAD_DIM, HEAD_DIM), F32),
            pltpu.VMEM((n_v, n_chunks, SUBLANES, HEAD_DIM), F32),
        ],
        compiler_params=_params(("parallel", "arbitrary")),
        name="gated_delta",
    )(proj, proj, proj, proj, bd, bd_t, norm_w.reshape(1, HEAD_DIM))


def _rope_tables(seq):
    inv = 1.0 / (ROPE_THETA ** (jnp.arange(0, HEAD_DIM, 2, dtype=F32) / HEAD_DIM))
    ang = jnp.arange(seq, dtype=F32)[:, None] * inv[None, :]
    ang = jnp.concatenate([ang, ang], axis=-1)
    sign = jnp.where(jnp.arange(HEAD_DIM) < HEAD_DIM // 2, -1.0, 1.0).astype(F32)
    return jnp.cos(ang), jnp.sin(ang) * sign[None, :]


def _diff_attention_mixer(h, xn, w_qkv, layer, lq1, lk1, lq2, lk2, subln, w_o, lambda_init,
                          batch, seq):
    d = h.shape[1]
    cos, sin = _rope_tables(seq)
    qkv = _proj_rope(xn, w_qkv, layer, cos, sin, seq, d, 2 * d, HEAD_DIM ** -0.5)
    lam_vecs = jnp.stack([lq1, lk1, lq2, lk2]).astype(F32)
    attn = _diff_attention(qkv, lam_vecs, subln, batch, seq, lambda_init)
    return _out_proj(attn, w_o, layer, h)


def _gated_deltanet_mixer(h, xn, w_in, conv_w, layer, a_log, dt_bias, gdn_norm, w_o, batch, seq):
    n_v_heads = a_log.shape[0]
    val_dim = n_v_heads * HEAD_DIM
    conv_dim = conv_w.shape[2]
    key_dim = (conv_dim - val_dim) // 2
    n_k_heads = key_dim // HEAD_DIM
    main = conv_dim + val_dim
    proj = _proj_conv(xn, w_in.astype(BF16), conv_w, layer, seq, main)
    bd, bd_t = _beta_decay(xn, w_in[layer, :, main:main + n_v_heads],
                           w_in[layer, :, main + n_v_heads:], a_log, dt_bias)
    o = _gated_delta(proj, bd, bd_t, gdn_norm, batch, seq, n_k_heads, n_v_heads)
    return _out_proj(o, w_o, layer, h)


def kernel(x, ffn1_norm, ffn1_w_gate, ffn1_w_up, ffn1_w_down, mix_norm, ffn2_norm, ffn2_w_gate, ffn2_w_up, ffn2_w_down, da_w_qkv, da_lambda_q1, da_lambda_k1, da_lambda_q2, da_lambda_k2, da_subln, da_w_o, gdn_w_in, gdn_conv_w, gdn_a_log, gdn_dt_bias, gdn_norm, gdn_w_o, final_norm):
    batch, seq, d = x.shape
    depth = ffn1_norm.shape[0]
    h = x.reshape(batch * seq, d)
    for i in range(depth):
        h, xn = _ffn(h, ffn1_norm[i], ffn1_w_gate, ffn1_w_up, ffn1_w_down, i, "next", mix_norm[i])
        j = i // 2
        if i % 2 == 0:
            lambda_init = 0.8 - 0.6 * math.exp(-0.3 * i)
            h = _diff_attention_mixer(h, xn, da_w_qkv, j, da_lambda_q1[j], da_lambda_k1[j],
                                      da_lambda_q2[j], da_lambda_k2[j], da_subln[j], da_w_o,
                                      lambda_init, batch, seq)
        else:
            h = _gated_deltanet_mixer(h, xn, gdn_w_in, gdn_conv_w, j, gdn_a_log[j],
                                      gdn_dt_bias[j], gdn_norm[j], gdn_w_o, batch, seq)
        if i == depth - 1:
            h = _ffn(h, ffn2_norm[i], ffn2_w_gate, ffn2_w_up, ffn2_w_down, i, "final", final_norm)
        else:
            h = _ffn(h, ffn2_norm[i], ffn2_w_gate, ffn2_w_up, ffn2_w_down, i)
    return h.reshape(batch, seq, d)
```

```python
import functools
import math

import jax
import jax.numpy as jnp
from jax import lax
from jax.experimental import pallas as pl
from jax.experimental.pallas import tpu as pltpu

F32 = jnp.float32
BF16 = jnp.bfloat16

RMS_EPS = 1e-6
L2_EPS = 1e-6
ROPE_THETA = 10000.0
HEAD_DIM = 128
DA_V_DIM = 2 * HEAD_DIM
CHUNK = 64
CONV_WIDTH = 4
CONV_HALO = 16
SUBLANES = 8
MXU_COLS = 256
N_BETA_DECAY_LANES = 128
VMEM_LIMIT = 56 * 1024 * 1024
FFN_VMEM_LIMIT = 62 * 1024 * 1024
MASK_VALUE = -1e30


def _params(semantics, vmem_limit=VMEM_LIMIT):
    return pltpu.CompilerParams(dimension_semantics=semantics, vmem_limit_bytes=vmem_limit)


def _tile(total, preferred):
    tile = min(preferred, total)
    while total % tile:
        tile //= 2
    return tile


def _rms_normalize(x, w):
    ms = jnp.mean(x * x, axis=-1, keepdims=True)
    return x * lax.rsqrt(ms + RMS_EPS) * w


def _silu(x):
    return x * jax.nn.sigmoid(x)


def _dot(a, b):
    return jnp.dot(a, b, preferred_element_type=F32)


def _dot_nt(a, b):
    return lax.dot_general(a, b, (((1,), (1,)), ((), ())), preferred_element_type=F32)


def _dot_tn(a, b):
    return lax.dot_general(a, b, (((0,), (0,)), ((), ())), preferred_element_type=F32)


def _dot_exact(a, b):
    return jnp.dot(a, b, preferred_element_type=F32, precision=lax.Precision.HIGHEST)


def _ffn_body(h_ref, nw_ref, wg_ref, wu_ref, wd_ref, ew_ref, o_ref, xn_ref, *, epilogue):
    j = pl.program_id(1)

    @pl.when(j == 0)
    def _():
        xn_ref[...] = _rms_normalize(h_ref[...], nw_ref[...]).astype(BF16)
        o_ref[...] = jnp.zeros_like(o_ref)

    xn = xn_ref[...]
    g = _dot(xn, wg_ref[...].astype(BF16))
    u = _dot(xn, wu_ref[...].astype(BF16))
    a = (_silu(g) * u).astype(BF16)
    o_ref[...] += _dot(a, wd_ref[...].astype(BF16))

    @pl.when(j == pl.num_programs(1) - 1)
    def _():
        y = h_ref[...] + 0.5 * o_ref[...]
        if epilogue == "final":
            o_ref[...] = _rms_normalize(y, ew_ref[...])
        else:
            o_ref[...] = y
            if epilogue == "next":
                xn_ref[...] = _rms_normalize(y, ew_ref[...]).astype(BF16)


def _ffn(h, norm_w, w_gate, w_up, w_down, layer, epilogue=None, epilogue_w=None, *,
         tm=1024, tf=256):
    n, d = h.shape
    f = w_gate.shape[2]
    tm, tf = _tile(n, tm), _tile(f, tf)
    ew = norm_w if epilogue_w is None else epilogue_w
    row_block = pl.BlockSpec((tm, d), lambda i, j: (i, 0))
    out_shape = jax.ShapeDtypeStruct((n, d), F32)
    out_specs = row_block
    scratch_shapes = [pltpu.VMEM((tm, d), BF16)]
    if epilogue == "next":
        out_shape = (out_shape, jax.ShapeDtypeStruct((n, d), BF16))
        out_specs = (row_block, row_block)
        scratch_shapes = []
    return pl.pallas_call(
        functools.partial(_ffn_body, epilogue=epilogue),
        out_shape=out_shape,
        grid=(n // tm, f // tf),
        in_specs=[
            row_block,
            pl.BlockSpec((1, d), lambda i, j: (0, 0)),
            pl.BlockSpec((None, d, tf), lambda i, j: (layer, 0, j)),
            pl.BlockSpec((None, d, tf), lambda i, j: (layer, 0, j)),
            pl.BlockSpec((None, tf, d), lambda i, j: (layer, j, 0)),
            pl.BlockSpec((1, d), lambda i, j: (0, 0)),
        ],
        out_specs=out_specs,
        scratch_shapes=scratch_shapes,
        compiler_params=_params(("parallel", "arbitrary"), FFN_VMEM_LIMIT),
        name="ffn",
    )(h, norm_w.reshape(1, d), w_gate, w_up, w_down, ew.reshape(1, d))


def _proj_rope_body(x_ref, w_ref, cos_ref, sin_ref, o_ref, wb_ref, *,
                    n_q_tiles, n_rope_tiles, q_scale):
    j = pl.program_id(0)

    @pl.when(pl.program_id(1) == 0)
    def _():
        wb_ref[...] = w_ref[...].astype(BF16)

    tn = o_ref.shape[1]

    @pl.when(j < n_rope_tiles)
    def _():
        cos = cos_ref[...]
        sin = sin_ref[...]
        scale = jnp.where(j < n_q_tiles, q_scale, 1.0).astype(F32)
        for g in range(tn // MXU_COLS):
            y = _dot(x_ref[...], wb_ref[:, g * MXU_COLS:(g + 1) * MXU_COLS])
            for c in range(MXU_COLS // HEAD_DIM):
                x = y[:, c * HEAD_DIM:(c + 1) * HEAD_DIM]
                swapped = pltpu.roll(x, HEAD_DIM // 2, 1)
                lo = g * MXU_COLS + c * HEAD_DIM
                o_ref[:, lo:lo + HEAD_DIM] = (
                    (x * cos + swapped * sin) * scale).astype(o_ref.dtype)

    @pl.when(j >= n_rope_tiles)
    def _():
        o_ref[...] = _dot(x_ref[...], wb_ref[...]).astype(o_ref.dtype)


def _proj_rope(x, w, layer, cos, sin, seq, n_q_cols, n_rope_cols, q_scale, *, tm=1024, tn=1024):
    n, d = x.shape
    n_out = w.shape[2]
    tm, tn = _tile(seq, tm), _tile(n_q_cols, tn)
    assert n % tm == 0 and n_out % tn == 0 and n_rope_cols % tn == 0
    tiles_per_seq = seq // tm
    return pl.pallas_call(
        functools.partial(_proj_rope_body, n_q_tiles=n_q_cols // tn,
                          n_rope_tiles=n_rope_cols // tn, q_scale=q_scale),
        out_shape=jax.ShapeDtypeStruct((n, n_out), BF16),
        grid=(n_out // tn, n // tm),
        in_specs=[
            pl.BlockSpec((tm, d), lambda j, i: (i, 0)),
            pl.BlockSpec((None, d, tn), lambda j, i: (layer, 0, j)),
            pl.BlockSpec((tm, HEAD_DIM), lambda j, i: (i % tiles_per_seq, 0)),
            pl.BlockSpec((tm, HEAD_DIM), lambda j, i: (i % tiles_per_seq, 0)),
        ],
        out_specs=pl.BlockSpec((tm, tn), lambda j, i: (i, j)),
        scratch_shapes=[pltpu.VMEM((d, tn), BF16)],
        compiler_params=_params(("parallel", "arbitrary")),
        name="proj_rope",
    )(x, w, cos, sin)


def _proj_conv_body(x_ref, xh_ref, w_ref, cw_ref, o_ref, *, n_conv_tiles, tiles_per_seq):
    j = pl.program_id(0)
    i = pl.program_id(1)
    tm = x_ref.shape[0]
    keep = jnp.where(i % tiles_per_seq == 0, 0.0, 1.0).astype(BF16)

    @pl.when(j < n_conv_tiles)
    def _():
        lhs = jnp.concatenate([xh_ref[...] * keep, x_ref[...]], axis=0)
        for g in range(o_ref.shape[1] // MXU_COLS):
            cols = slice(g * MXU_COLS, (g + 1) * MXU_COLS)
            y = _dot(lhs, w_ref[:, cols])
            cw = cw_ref[:, cols]
            groups = y.shape[0] // SUBLANES
            halo_groups = CONV_HALO // SUBLANES
            y3 = y.reshape(groups, SUBLANES, MXU_COLS)
            sub = lax.broadcasted_iota(jnp.int32, (groups - halo_groups, SUBLANES, MXU_COLS), 1)
            acc = cw[CONV_WIDTH - 1:CONV_WIDTH, :][None] * y3[halo_groups:]
            rot = y3
            for back in range(1, CONV_WIDTH):
                tap = CONV_WIDTH - 1 - back
                rot = pltpu.roll(rot, 1, 1)
                shifted = jnp.where(sub >= back, rot[halo_groups:], rot[halo_groups - 1:-1])
                acc = acc + cw[tap:tap + 1, :][None] * shifted
            o_ref[:, cols] = _silu(acc).reshape(tm, MXU_COLS).astype(o_ref.dtype)

    @pl.when(j >= n_conv_tiles)
    def _():
        o_ref[...] = _dot(x_ref[...], w_ref[...]).astype(o_ref.dtype)


def _proj_conv(x, w, conv_w, layer, seq, n_out, *, tm=1024, tn=2048):
    assert w.dtype == BF16
    n, d = x.shape
    n_conv = conv_w.shape[2]
    tm, tn = _tile(seq, tm), _tile(math.gcd(n_conv, n_out), tn)
    assert n % tm == 0 and tm % CONV_HALO == 0
    assert CONV_WIDTH - 1 < SUBLANES <= CONV_HALO and CONV_HALO % SUBLANES == 0
    tiles_per_seq = seq // tm
    n_conv_tiles = n_conv // tn
    halo_blocks = tm // CONV_HALO
    return pl.pallas_call(
        functools.partial(_proj_conv_body, n_conv_tiles=n_conv_tiles,
                          tiles_per_seq=tiles_per_seq),
        out_shape=jax.ShapeDtypeStruct((n, n_out), BF16),
        grid=(n_out // tn, n // tm),
        in_specs=[
            pl.BlockSpec((tm, d), lambda j, i: (i, 0)),
            pl.BlockSpec((CONV_HALO, d), lambda j, i: (jnp.maximum(i * halo_blocks - 1, 0), 0)),
            pl.BlockSpec((None, d, tn), lambda j, i: (layer, 0, j)),
            pl.BlockSpec((None, CONV_WIDTH, tn),
                         lambda j, i: (layer, 0, jnp.minimum(j, n_conv_tiles - 1))),
        ],
        out_specs=pl.BlockSpec((tm, tn), lambda j, i: (i, j)),
        compiler_params=_params(("parallel", "parallel")),
        name="proj_conv",
    )(x, x, w, conv_w)


def _out_proj_body(x_ref, w_ref, r_ref, o_ref, wb_ref):
    @pl.when(pl.program_id(1) == 0)
    def _():
        wb_ref[...] = w_ref[...].astype(BF16)

    o_ref[...] = r_ref[...] + _dot(x_ref[...], wb_ref[...])


def _out_proj(x, w, layer, res, *, tn=1024, x_tile_bytes=4 << 20, w_double_buffer_bytes=8 << 20):
    n, k = x.shape
    d = w.shape[2]
    tn = _tile(d, tn)
    tm = _tile(n, x_tile_bytes // (k * x.dtype.itemsize))
    w_spec = pl.BlockSpec((None, k, tn), lambda j, i: (layer, 0, j))
    if k * tn * w.dtype.itemsize > w_double_buffer_bytes:
        w_spec = pl.BlockSpec((None, k, tn), lambda j, i: (layer, 0, j),
                              pipeline_mode=pl.Buffered(1))
    return pl.pallas_call(
        _out_proj_body,
        out_shape=jax.ShapeDtypeStruct((n, d), F32),
        grid=(d // tn, n // tm),
        in_specs=[
            pl.BlockSpec((tm, k), lambda j, i: (i, 0)),
            w_spec,
            pl.BlockSpec((tm, tn), lambda j, i: (i, j)),
        ],
        out_specs=pl.BlockSpec((tm, tn), lambda j, i: (i, j)),
        scratch_shapes=[pltpu.VMEM((k, tn), BF16)],
        compiler_params=_params(("parallel", "arbitrary")),
        name="out_proj",
    )(x, w, res)


def _diff_attn_body(lam_ref, subln_ref, q_ref, k_ref, v_ref, o_ref, *, tq, nq, lambda_init):
    i = pl.program_id(2)
    lv = lam_ref[...]
    lam = (jnp.exp(jnp.sum(lv[0:1] * lv[1:2], axis=-1, keepdims=True))
           - jnp.exp(jnp.sum(lv[2:3] * lv[3:4], axis=-1, keepdims=True)) + lambda_init)
    key_pos = lax.broadcasted_iota(jnp.int32, (tq, tq), 0)
    query_pos = lax.broadcasted_iota(jnp.int32, (tq, tq), 1)
    causal = key_pos <= query_pos

    def attend_both(n_full):
        diag = slice(n_full, n_full + tq)
        cols = [slice(sub * HEAD_DIM, (sub + 1) * HEAD_DIM) for sub in (0, 1)]
        qs = [q_ref[:, c] for c in cols]
        s_diag = [jnp.where(causal, _dot_nt(k_ref[diag, c], q), MASK_VALUE)
                  for c, q in zip(cols, qs)]
        s_full = [_dot_nt(k_ref[0:n_full, c], q) for c, q in zip(cols, qs)] if n_full else None
        outs = []
        for sub in (0, 1):
            m = jnp.max(s_diag[sub], axis=0, keepdims=True)
            if n_full:
                m = jnp.maximum(m, jnp.max(s_full[sub], axis=0, keepdims=True))
            p_diag = jnp.exp(s_diag[sub] - m)
            l = jnp.sum(p_diag, axis=0, keepdims=True)
            acc = _dot_tn(v_ref[diag, :], p_diag.astype(BF16))
            if n_full:
                p_full = jnp.exp(s_full[sub] - m)
                l = l + jnp.sum(p_full, axis=0, keepdims=True)
                acc = acc + _dot_tn(v_ref[0:n_full, :], p_full.astype(BF16))
            outs.append(acc / l)
        return outs

    for c in range(nq):
        @pl.when(i == c)
        def _(c=c):
            o0, o1 = attend_both(c * tq)
            o = o0 - lam * o1
            ms = jnp.mean(o * o, axis=0, keepdims=True)
            y = o * lax.rsqrt(ms + RMS_EPS) * subln_ref[...] * (1.0 - lambda_init)
            o_ref[...] = y.T.astype(o_ref.dtype)


def _diff_attention(qkv, lam_vecs, subln, batch, seq, lambda_init, *, tq=512):
    n, width = qkv.shape
    d = width // 3
    heads = d // DA_V_DIM
    tq = _tile(seq, tq)
    nq = seq // tq
    return pl.pallas_call(
        functools.partial(_diff_attn_body, tq=tq, nq=nq, lambda_init=lambda_init),
        out_shape=jax.ShapeDtypeStruct((n, d), BF16),
        grid=(batch, heads, nq),
        in_specs=[
            pl.BlockSpec((4, HEAD_DIM), lambda b, h, i: (0, 0)),
            pl.BlockSpec((DA_V_DIM, 1), lambda b, h, i: (0, 0)),
            pl.BlockSpec((tq, DA_V_DIM), lambda b, h, i: (b * nq + i, h)),
            pl.BlockSpec((seq, DA_V_DIM), lambda b, h, i: (b, heads + h)),
            pl.BlockSpec((seq, DA_V_DIM), lambda b, h, i: (b, 2 * heads + h)),
        ],
        out_specs=pl.BlockSpec((tq, DA_V_DIM), lambda b, h, i: (b * nq + i, h)),
        compiler_params=_params(("parallel", "parallel", "arbitrary")),
        name="diff_attn",
    )(lam_vecs, subln.reshape(DA_V_DIM, 1), qkv, qkv, qkv)


def _beta_decay_body(x_ref, w_ref, alog_ref, dt_ref, o_ref, ot_ref, *, n_heads):
    tm = x_ref.shape[0]
    y = _dot(x_ref[...], w_ref[...])
    a = y + dt_ref[...]
    softplus = jnp.maximum(a, 0.0) + jnp.log1p(jnp.exp(-jnp.abs(a)))
    g = -jnp.exp(alog_ref[...]) * softplus
    row = lax.broadcasted_iota(jnp.int32, (tm, tm), 0)
    col = lax.broadcasted_iota(jnp.int32, (tm, tm), 1)
    same_chunk = (row // CHUNK) == (col // CHUNK)
    prefix = jnp.where(same_chunk & (col <= row), 1.0, 0.0).astype(F32)
    whole = jnp.where(same_chunk, 1.0, 0.0).astype(F32)
    g_cum = _dot_exact(prefix, g)
    g_all = _dot_exact(whole, g)
    lane = lax.broadcasted_iota(jnp.int32, y.shape, 1)
    packed = jnp.where(lane < n_heads, jax.nn.sigmoid(y),
                       jnp.where(lane < 2 * n_heads, g_cum,
                                 jnp.where(lane < 3 * n_heads, g_all, 0.0)))
    o_ref[...] = packed
    ot_ref[...] = packed.T


def _beta_decay(x, w_b, w_a, a_log, dt_bias, *, tm=256):
    n, d = x.shape
    n_heads = a_log.shape[0]
    lanes = N_BETA_DECAY_LANES
    assert n % tm == 0 and tm % CHUNK == 0 and 3 * n_heads <= lanes
    pad = lanes - 3 * n_heads
    w = jnp.concatenate([w_b, w_a, w_a, jnp.zeros((d, pad), w_a.dtype)], axis=1).astype(BF16)
    zeros = jnp.zeros((n_heads,), F32)
    a_log_l = jnp.concatenate([zeros, a_log, a_log, jnp.zeros((pad,), F32)]).reshape(1, lanes)
    dt_l = jnp.concatenate([zeros, dt_bias, dt_bias, jnp.zeros((pad,), F32)]).reshape(1, lanes)
    return pl.pallas_call(
        functools.partial(_beta_decay_body, n_heads=n_heads),
        out_shape=(jax.ShapeDtypeStruct((n, lanes), F32), jax.ShapeDtypeStruct((lanes, n), F32)),
        grid=(n // tm,),
        in_specs=[
            pl.BlockSpec((tm, d), lambda i: (i, 0)),
            pl.BlockSpec((d, lanes), lambda i: (0, 0)),
            pl.BlockSpec((1, lanes), lambda i: (0, 0)),
            pl.BlockSpec((1, lanes), lambda i: (0, 0)),
        ],
        out_specs=(pl.BlockSpec((tm, lanes), lambda i: (i, 0)),
                   pl.BlockSpec((lanes, tm), lambda i: (0, i))),
        compiler_params=_params(("parallel",)),
        name="beta_decay",
    )(x, w, a_log_l, dt_l)


def _delta_body(q_ref, k_ref, v_ref, z_ref, bd_ref, bdt_ref, nw_ref, o_ref,
                state_ref, qeff_ref, oloc_ref, m_ref, n_ref, g_ref, *,
                tb, n_heads, rep, khs, n_groups):
    s = pl.program_id(1)
    n_pairs = pl.num_programs(1) - 1
    n_v = khs * rep
    n_chunks = tb // CHUNK
    lanes = N_BETA_DECAY_LANES
    prev_group = jnp.maximum(s - 1, 0) % n_groups
    hv0 = (jnp.minimum(s, n_pairs - 1) % n_groups) * n_v

    @pl.when(s == 0)
    def _():
        state_ref[...] = jnp.zeros_like(state_ref)
        qeff_ref[...] = jnp.zeros_like(qeff_ref)
        oloc_ref[...] = jnp.zeros_like(oloc_ref)
        m_ref[...] = jnp.zeros_like(m_ref)
        n_ref[...] = jnp.zeros_like(n_ref)
        g_ref[...] = jnp.zeros_like(g_ref)

    def head_slice(x, i):
        return x[:, i * HEAD_DIM:(i + 1) * HEAD_DIM]

    states = [state_ref[prev_group * n_v + idx] for idx in range(n_v)]
    outs = [[] for _ in range(n_v)]
    for c in range(n_chunks):
        rows = slice(c * CHUNK, (c + 1) * CHUNK)
        for idx in range(n_v):
            lhs = jnp.concatenate([m_ref[idx, c], qeff_ref[idx, rows, :]], axis=0)
            ms = _dot(lhs, states[idx].astype(BF16))
            outs[idx].append(ms[HEAD_DIM:] + oloc_ref[idx, rows, :])
            states[idx] = (states[idx] * g_ref[idx, c][0:1, :] - ms[:HEAD_DIM] + n_ref[idx, c])
    nw = nw_ref[...]
    for idx in range(n_v):
        state_ref[prev_group * n_v + idx] = states[idx]
        o = jnp.concatenate(outs[idx], axis=0)
        z = head_slice(z_ref, idx).astype(F32)
        o = o * lax.rsqrt(jnp.mean(o * o, axis=-1, keepdims=True) + RMS_EPS) * nw * _silu(z)
        o_ref[:, idx * HEAD_DIM:(idx + 1) * HEAD_DIM] = o.astype(o_ref.dtype)

    qs, ks, kks, qks = [], [], [], []
    for a in range(khs):
        q = head_slice(q_ref, a).astype(F32)
        k = head_slice(k_ref, a).astype(F32)
        q = q * lax.rsqrt(jnp.sum(q * q, axis=-1, keepdims=True) + L2_EPS) * (HEAD_DIM ** -0.5)
        k = k * lax.rsqrt(jnp.sum(k * k, axis=-1, keepdims=True) + L2_EPS)
        qs.append(q)
        ks.append(k)
        k_b = k.astype(BF16)
        kks.append(_dot_nt(k_b, k_b))
        qks.append(_dot_nt(q.astype(BF16), k_b))

    row = lax.broadcasted_iota(jnp.int32, (tb, tb), 0)
    col = lax.broadcasted_iota(jnp.int32, (tb, tb), 1)
    same_chunk = (row // CHUNK) == (col // CHUNK)
    tril = same_chunk & (col <= row)
    strict = same_chunk & (col < row)

    bd = pltpu.roll(bd_ref[...], (lanes - hv0) % lanes, 1)

    p_cur, rhs, qk_d, q_g, k_g, g_chunk = [], [], [], [], [], []
    for idx in range(n_v):
        a = idx // rep
        beta_c = bd[:, idx:idx + 1]
        gcum_c = bd[:, n_heads + idx:n_heads + idx + 1]
        gall_c = bd[:, 2 * n_heads + idx:2 * n_heads + idx + 1]
        gcum_r = bdt_ref[pl.ds(n_heads + hv0 + idx, 1), :]
        e = jnp.exp(gcum_c - gcum_r)
        p_cur.append((kks[a] * jnp.where(strict, e, 0.0) * (-beta_c)).astype(BF16))
        qk_d.append((qks[a] * jnp.where(tril, e, 0.0)).astype(BF16))
        exp_g = jnp.exp(gcum_c)
        rhs.append(jnp.concatenate([head_slice(v_ref, idx).astype(F32) * beta_c,
                                    ks[a] * (beta_c * exp_g)], axis=-1))
        q_g.append(qs[a] * exp_g)
        k_g.append((ks[a] * jnp.exp(gall_c - gcum_c)).astype(BF16))
        g_chunk.append(jnp.exp(gall_c))

    n_factors = CHUNK.bit_length() - 1
    blk = HEAD_DIM
    n_blk = tb // blk
    p_cur = [[p[h * blk:(h + 1) * blk, h * blk:(h + 1) * blk] for h in range(n_blk)]
             for p in p_cur]
    rhs = [[r[h * blk:(h + 1) * blk] for h in range(n_blk)] for r in rhs]
    for j in range(n_factors):
        for idx in range(n_v):
            for h in range(n_blk):
                p_b = p_cur[idx][h]
                r_b = rhs[idx][h].astype(BF16)
                if j < n_factors - 1:
                    res = _dot(p_b, jnp.concatenate([p_b, r_b], axis=-1))
                    p_cur[idx][h] = res[:, :blk].astype(BF16)
                    rhs[idx][h] = rhs[idx][h] + res[:, blk:]
                else:
                    rhs[idx][h] = rhs[idx][h] + _dot(p_b, r_b)
    rhs = [jnp.concatenate(r, axis=0) for r in rhs]

    for idx in range(n_v):
        uw_b = rhs[idx].astype(BF16)
        quw = _dot(qk_d[idx], uw_b)
        oloc_ref[idx] = quw[:, :HEAD_DIM]
        qeff_ref[idx] = (q_g[idx] - quw[:, HEAD_DIM:]).astype(BF16)
        for c in range(n_chunks):
            rows = slice(c * CHUNK, (c + 1) * CHUNK)
            kuw = _dot_tn(k_g[idx][rows], uw_b[rows])
            n_ref[idx, c] = kuw[:, :HEAD_DIM]
            m_ref[idx, c] = kuw[:, HEAD_DIM:].astype(BF16)
            g_ref[idx, c] = jnp.broadcast_to(g_chunk[idx][c * CHUNK:c * CHUNK + 1, :],
                                             (SUBLANES, HEAD_DIM))


def _gated_delta(proj, bd, bd_t, norm_w, batch, seq, n_k_heads, n_v_heads, *, tb=256, khs=4):
    n = proj.shape[0]
    rep = n_v_heads // n_k_heads
    khs = _tile(n_k_heads, khs)
    n_groups = n_k_heads // khs
    n_v = khs * rep
    kw = khs * HEAD_DIM
    vw = n_v * HEAD_DIM
    key_dim = n_k_heads * HEAD_DIM
    val_dim = n_v_heads * HEAD_DIM
    assert seq % tb == 0 and tb % HEAD_DIM == 0 and HEAD_DIM % CHUNK == 0
    nt = seq // tb
    n_chunks = tb // CHUNK
    n_pairs = nt * n_groups
    k_blk = key_dim // kw
    v_blk = 2 * key_dim // vw
    z_blk = (2 * key_dim + val_dim) // vw
    lanes = N_BETA_DECAY_LANES

    def prep(s):
        pair = jnp.minimum(s, n_pairs - 1)
        return pair // n_groups, pair % n_groups

    def fin(s):
        pair = jnp.maximum(s - 1, 0)
        return pair // n_groups, pair % n_groups

    return pl.pallas_call(
        functools.partial(_delta_body, tb=tb, n_heads=n_v_heads, rep=rep, khs=khs,
                          n_groups=n_groups),
        out_shape=jax.ShapeDtypeStruct((n, val_dim), BF16),
        grid=(batch, n_pairs + 1),
        in_specs=[
            pl.BlockSpec((tb, kw), lambda b, s: (b * nt + prep(s)[0], prep(s)[1])),
            pl.BlockSpec((tb, kw), lambda b, s: (b * nt + prep(s)[0], k_blk + prep(s)[1])),
            pl.BlockSpec((tb, vw), lambda b, s: (b * nt + prep(s)[0], v_blk + prep(s)[1])),
            pl.BlockSpec((tb, vw), lambda b, s: (b * nt + fin(s)[0], z_blk + fin(s)[1])),
            pl.BlockSpec((tb, lanes), lambda b, s: (b * nt + prep(s)[0], 0)),
            pl.BlockSpec((lanes, tb), lambda b, s: (0, b * nt + prep(s)[0])),
            pl.BlockSpec((1, HEAD_DIM), lambda b, s: (0, 0)),
        ],
        out_specs=pl.BlockSpec((tb, vw), lambda b, s: (b * nt + fin(s)[0], fin(s)[1])),
        scratch_shapes=[
            pltpu.VMEM((n_v_heads, HEAD_DIM, HEAD_DIM), F32),
            pltpu.VMEM((n_v, tb, HEAD_DIM), BF16),
            pltpu.VMEM((n_v, tb, HEAD_DIM), F32),
            pltpu.VMEM((n_v, n_chunks, HEAD_DIM, HEAD_DIM), BF16),
            pltpu.VMEM((n_v, n_chunks, HEAD_DIM, HEAD_DIM), F32),
            pltpu.VMEM((n_v, n_chunks, SUBLANES, HEAD_DIM), F32),
        ],
        compiler_params=_params(("parallel", "arbitrary")),
        name="gated_delta",
    )(proj, proj, proj, proj, bd, bd_t, norm_w.reshape(1, HEAD_DIM))


def _rope_tables(seq):
    inv = 1.0 / (ROPE_THETA ** (jnp.arange(0, HEAD_DIM, 2, dtype=F32) / HEAD_DIM))
    ang = jnp.arange(seq, dtype=F32)[:, None] * inv[None, :]
    ang = jnp.concatenate([ang, ang], axis=-1)
    sign = jnp.where(jnp.arange(HEAD_DIM) < HEAD_DIM // 2, -1.0, 1.0).astype(F32)
    return jnp.cos(ang), jnp.sin(ang) * sign[None, :]


def _diff_attention_mixer(h, xn, w_qkv, layer, lq1, lk1, lq2, lk2, subln, w_o, lambda_init,
                          batch, seq):
    d = h.shape[1]
    cos, sin = _rope_tables(seq)
    qkv = _proj_rope(xn, w_qkv, layer, cos, sin, seq, d, 2 * d, HEAD_DIM ** -0.5)
    lam_vecs = jnp.stack([lq1, lk1, lq2, lk2]).astype(F32)
    attn = _diff_attention(qkv, lam_vecs, subln, batch, seq, lambda_init)
    return _out_proj(attn, w_o, layer, h)


def _gated_deltanet_mixer(h, xn, w_in, conv_w, layer, a_log, dt_bias, gdn_norm, w_o, batch, seq):
    n_v_heads = a_log.shape[0]
    val_dim = n_v_heads * HEAD_DIM
    conv_dim = conv_w.shape[2]
    key_dim = (conv_dim - val_dim) // 2
    n_k_heads = key_dim // HEAD_DIM
    main = conv_dim + val_dim
    proj = _proj_conv(xn, w_in.astype(BF16), conv_w, layer, seq, main)
    bd, bd_t = _beta_decay(xn, w_in[layer, :, main:main + n_v_heads],
                           w_in[layer, :, main + n_v_heads:], a_log, dt_bias)
    o = _gated_delta(proj, bd, bd_t, gdn_norm, batch, seq, n_k_heads, n_v_heads)
    return _out_proj(o, w_o, layer, h)


def kernel(x, ffn1_norm, ffn1_w_gate, ffn1_w_up, ffn1_w_down, mix_norm, ffn2_norm, ffn2_w_gate, ffn2_w_up, ffn2_w_down, da_w_qkv, da_lambda_q1, da_lambda_k1, da_lambda_q2, da_lambda_k2, da_subln, da_w_o, gdn_w_in, gdn_conv_w, gdn_a_log, gdn_dt_bias, gdn_norm, gdn_w_o, final_norm):
    batch, seq, d = x.shape
    depth = ffn1_norm.shape[0]
    h = x.reshape(batch * seq, d)
    for i in range(depth):
        h, xn = _ffn(h, ffn1_norm[i], ffn1_w_gate, ffn1_w_up, ffn1_w_down, i, "next", mix_norm[i])
        j = i // 2
        if i % 2 == 0:
            lambda_init = 0.8 - 0.6 * math.exp(-0.3 * i)
            h = _diff_attention_mixer(h, xn, da_w_qkv, j, da_lambda_q1[j], da_lambda_k1[j],
                                      da_lambda_q2[j], da_lambda_k2[j], da_subln[j], da_w_o,
                                      lambda_init, batch, seq)
        else:
            h = _gated_deltanet_mixer(h, xn, gdn_w_in, gdn_conv_w, j, gdn_a_log[j],
                                      gdn_dt_bias[j], gdn_norm[j], gdn_w_o, batch, seq)
        if i == depth - 1:
            h = _ffn(h, ffn2_norm[i], ffn2_w_gate, ffn2_w_up, ffn2_w_down, i, "final", final_norm)
        else:
            h = _ffn(h, ffn2_norm[i], ffn2_w_gate, ffn2_w_up, ffn2_w_down, i)
    return h.reshape(batch, seq, d)
```

```python
import functools
import math

import jax
import jax.numpy as jnp
from jax import lax
from jax.experimental import pallas as pl
from jax.experimental.pallas import tpu as pltpu

F32 = jnp.float32
BF16 = jnp.bfloat16

RMS_EPS = 1e-6
L2_EPS = 1e-6
ROPE_THETA = 10000.0
HEAD_DIM = 128
DA_V_DIM = 2 * HEAD_DIM
CHUNK = 64
CONV_WIDTH = 4
CONV_HALO = 16
SUBLANES = 8
MXU_COLS = 256
N_BETA_DECAY_LANES = 128
VMEM_LIMIT = 56 * 1024 * 1024
FFN_VMEM_LIMIT = 62 * 1024 * 1024
MASK_VALUE = -1e30


def _params(semantics, vmem_limit=VMEM_LIMIT):
    return pltpu.CompilerParams(dimension_semantics=semantics, vmem_limit_bytes=vmem_limit)


def _tile(total, preferred):
    tile = min(preferred, total)
    while total % tile:
        tile //= 2
    return tile


def _rms_normalize(x, w):
    ms = jnp.mean(x * x, axis=-1, keepdims=True)
    return x * lax.rsqrt(ms + RMS_EPS) * w


def _silu(x):
    return x * jax.nn.sigmoid(x)


def _dot(a, b):
    return jnp.dot(a, b, preferred_element_type=F32)


def _dot_nt(a, b):
    return lax.dot_general(a, b, (((1,), (1,)), ((), ())), preferred_element_type=F32)


def _dot_tn(a, b):
    return lax.dot_general(a, b, (((0,), (0,)), ((), ())), preferred_element_type=F32)


def _ffn_body(h_ref, nw_ref, wg_ref, wu_ref, wd_ref, ew_ref, o_ref, xn_ref, *, epilogue):
    j = pl.program_id(1)

    @pl.when(j == 0)
    def _():
        xn_ref[...] = _rms_normalize(h_ref[...], nw_ref[...]).astype(BF16)
        o_ref[...] = jnp.zeros_like(o_ref)

    xn = xn_ref[...]
    g = _dot(xn, wg_ref[...].astype(BF16))
    u = _dot(xn, wu_ref[...].astype(BF16))
    a = (_silu(g) * u).astype(BF16)
    o_ref[...] += _dot(a, wd_ref[...].astype(BF16))

    @pl.when(j == pl.num_programs(1) - 1)
    def _():
        y = h_ref[...] + 0.5 * o_ref[...]
        if epilogue == "final":
            o_ref[...] = _rms_normalize(y, ew_ref[...])
        else:
            o_ref[...] = y
            if epilogue == "next":
                xn_ref[...] = _rms_normalize(y, ew_ref[...]).astype(BF16)


def _ffn(h, norm_w, w_gate, w_up, w_down, layer, epilogue=None, epilogue_w=None, *,
         tm=1024, tf=256):
    n, d = h.shape
    f = w_gate.shape[2]
    tm, tf = _tile(n, tm), _tile(f, tf)
    ew = norm_w if epilogue_w is None else epilogue_w
    row_block = pl.BlockSpec((tm, d), lambda i, j: (i, 0))
    out_shape = jax.ShapeDtypeStruct((n, d), F32)
    out_specs = row_block
    scratch_shapes = [pltpu.VMEM((tm, d), BF16)]
    if epilogue == "next":
        out_shape = (out_shape, jax.ShapeDtypeStruct((n, d), BF16))
        out_specs = (row_block, row_block)
        scratch_shapes = []
    return pl.pallas_call(
        functools.partial(_ffn_body, epilogue=epilogue),
        out_shape=out_shape,
        grid=(n // tm, f // tf),
        in_specs=[
            row_block,
            pl.BlockSpec((1, d), lambda i, j: (0, 0)),
            pl.BlockSpec((None, d, tf), lambda i, j: (layer, 0, j)),
            pl.BlockSpec((None, d, tf), lambda i, j: (layer, 0, j)),
            pl.BlockSpec((None, tf, d), lambda i, j: (layer, j, 0)),
            pl.BlockSpec((1, d), lambda i, j: (0, 0)),
        ],
        out_specs=out_specs,
        scratch_shapes=scratch_shapes,
        compiler_params=_params(("parallel", "arbitrary"), FFN_VMEM_LIMIT),
        name="ffn",
    )(h, norm_w.reshape(1, d), w_gate, w_up, w_down, ew.reshape(1, d))


def _proj_rope_body(x_ref, w_ref, cos_ref, sin_ref, o_ref, wb_ref, *,
                    n_q_tiles, n_rope_tiles, q_scale):
    j = pl.program_id(0)

    @pl.when(pl.program_id(1) == 0)
    def _():
        wb_ref[...] = w_ref[...].astype(BF16)

    tn = o_ref.shape[1]

    @pl.when(j < n_rope_tiles)
    def _():
        cos = cos_ref[...]
        sin = sin_ref[...]
        scale = jnp.where(j < n_q_tiles, q_scale, 1.0).astype(F32)
        for g in range(tn // MXU_COLS):
            y = _dot(x_ref[...], wb_ref[:, g * MXU_COLS:(g + 1) * MXU_COLS])
            for c in range(MXU_COLS // HEAD_DIM):
                x = y[:, c * HEAD_DIM:(c + 1) * HEAD_DIM]
                swapped = pltpu.roll(x, HEAD_DIM // 2, 1)
                lo = g * MXU_COLS + c * HEAD_DIM
                o_ref[:, lo:lo + HEAD_DIM] = (
                    (x * cos + swapped * sin) * scale).astype(o_ref.dtype)

    @pl.when(j >= n_rope_tiles)
    def _():
        o_ref[...] = _dot(x_ref[...], wb_ref[...]).astype(o_ref.dtype)


def _proj_rope(x, w, layer, cos, sin, seq, n_q_cols, n_rope_cols, q_scale, *, tm=1024, tn=1024):
    n, d = x.shape
    n_out = w.shape[2]
    tm, tn = _tile(seq, tm), _tile(n_q_cols, tn)
    assert n % tm == 0 and n_out % tn == 0 and n_rope_cols % tn == 0
    tiles_per_seq = seq // tm
    return pl.pallas_call(
        functools.partial(_proj_rope_body, n_q_tiles=n_q_cols // tn,
                          n_rope_tiles=n_rope_cols // tn, q_scale=q_scale),
        out_shape=jax.ShapeDtypeStruct((n, n_out), BF16),
        grid=(n_out // tn, n // tm),
        in_specs=[
            pl.BlockSpec((tm, d), lambda j, i: (i, 0)),
            pl.BlockSpec((None, d, tn), lambda j, i: (layer, 0, j)),
            pl.BlockSpec((tm, HEAD_DIM), lambda j, i: (i % tiles_per_seq, 0)),
            pl.BlockSpec((tm, HEAD_DIM), lambda j, i: (i % tiles_per_seq, 0)),
        ],
        out_specs=pl.BlockSpec((tm, tn), lambda j, i: (i, j)),
        scratch_shapes=[pltpu.VMEM((d, tn), BF16)],
        compiler_params=_params(("parallel", "arbitrary")),
        name="proj_rope",
    )(x, w, cos, sin)


def _proj_conv_body(x_ref, xh_ref, w_ref, cw_ref, o_ref, *, n_conv_tiles, tiles_per_seq):
    j = pl.program_id(0)
    i = pl.program_id(1)
    tm = x_ref.shape[0]
    keep = jnp.where(i % tiles_per_seq == 0, 0.0, 1.0).astype(BF16)

    @pl.when(j < n_conv_tiles)
    def _():
        lhs = jnp.concatenate([xh_ref[...] * keep, x_ref[...]], axis=0)
        for g in range(o_ref.shape[1] // MXU_COLS):
            cols = slice(g * MXU_COLS, (g + 1) * MXU_COLS)
            y = _dot(lhs, w_ref[:, cols])
            cw = cw_ref[:, cols]
            groups = y.shape[0] // SUBLANES
            halo_groups = CONV_HALO // SUBLANES
            y3 = y.reshape(groups, SUBLANES, MXU_COLS)
            sub = lax.broadcasted_iota(jnp.int32, (groups - halo_groups, SUBLANES, MXU_COLS), 1)
            acc = cw[CONV_WIDTH - 1:CONV_WIDTH, :][None] * y3[halo_groups:]
            rot = y3
            for back in range(1, CONV_WIDTH):
                tap = CONV_WIDTH - 1 - back
                rot = pltpu.roll(rot, 1, 1)
                shifted = jnp.where(sub >= back, rot[halo_groups:], rot[halo_groups - 1:-1])
                acc = acc + cw[tap:tap + 1, :][None] * shifted
            o_ref[:, cols] = _silu(acc).reshape(tm, MXU_COLS).astype(o_ref.dtype)

    @pl.when(j >= n_conv_tiles)
    def _():
        o_ref[...] = _dot(x_ref[...], w_ref[...]).astype(o_ref.dtype)


def _proj_conv(x, w, conv_w, layer, seq, n_out, *, tm=1024, tn=2048):
    assert w.dtype == BF16
    n, d = x.shape
    n_conv = conv_w.shape[2]
    tm, tn = _tile(seq, tm), _tile(math.gcd(n_conv, n_out), tn)
    assert n % tm == 0 and tm % CONV_HALO == 0
    assert CONV_WIDTH - 1 < SUBLANES <= CONV_HALO and CONV_HALO % SUBLANES == 0
    tiles_per_seq = seq // tm
    n_conv_tiles = n_conv // tn
    halo_blocks = tm // CONV_HALO
    return pl.pallas_call(
        functools.partial(_proj_conv_body, n_conv_tiles=n_conv_tiles,
                          tiles_per_seq=tiles_per_seq),
        out_shape=jax.ShapeDtypeStruct((n, n_out), BF16),
        grid=(n_out // tn, n // tm),
        in_specs=[
            pl.BlockSpec((tm, d), lambda j, i: (i, 0)),
            pl.BlockSpec((CONV_HALO, d), lambda j, i: (jnp.maximum(i * halo_blocks - 1, 0), 0)),
            pl.BlockSpec((None, d, tn), lambda j, i: (layer, 0, j)),
            pl.BlockSpec((None, CONV_WIDTH, tn),
                         lambda j, i: (layer, 0, jnp.minimum(j, n_conv_tiles - 1))),
        ],
        out_specs=pl.BlockSpec((tm, tn), lambda j, i: (i, j)),
        compiler_params=_params(("parallel", "parallel")),
        name="proj_conv",
    )(x, x, w, conv_w)


def _out_proj_body(x_ref, w_ref, r_ref, o_ref, wb_ref):
    @pl.when(pl.program_id(1) == 0)
    def _():
        wb_ref[...] = w_ref[...].astype(BF16)

    o_ref[...] = r_ref[...] + _dot(x_ref[...], wb_ref[...])


def _out_proj(x, w, layer, res, *, tn=1024, x_tile_bytes=4 << 20, w_double_buffer_bytes=8 << 20):
    n, k = x.shape
    d = w.shape[2]
    tn = _tile(d, tn)
    tm = _tile(n, x_tile_bytes // (k * x.dtype.itemsize))
    w_spec = pl.BlockSpec((None, k, tn), lambda j, i: (layer, 0, j))
    if k * tn * w.dtype.itemsize > w_double_buffer_bytes:
        w_spec = pl.BlockSpec((None, k, tn), lambda j, i: (layer, 0, j),
                              pipeline_mode=pl.Buffered(1))
    return pl.pallas_call(
        _out_proj_body,
        out_shape=jax.ShapeDtypeStruct((n, d), F32),
        grid=(d // tn, n // tm),
        in_specs=[
            pl.BlockSpec((tm, k), lambda j, i: (i, 0)),
            w_spec,
            pl.BlockSpec((tm, tn), lambda j, i: (i, j)),
        ],
        out_specs=pl.BlockSpec((tm, tn), lambda j, i: (i, j)),
        scratch_shapes=[pltpu.VMEM((k, tn), BF16)],
        compiler_params=_params(("parallel", "arbitrary")),
        name="out_proj",
    )(x, w, res)


def _diff_attn_body(lam_ref, subln_ref, q_ref, k_ref, v_ref, o_ref, *, tq, nq, lambda_init):
    i = pl.program_id(2)
    lv = lam_ref[...]
    lam = (jnp.exp(jnp.sum(lv[0:1] * lv[1:2], axis=-1, keepdims=True))
           - jnp.exp(jnp.sum(lv[2:3] * lv[3:4], axis=-1, keepdims=True)) + lambda_init)
    key_pos = lax.broadcasted_iota(jnp.int32, (tq, tq), 0)
    query_pos = lax.broadcasted_iota(jnp.int32, (tq, tq), 1)
    causal = key_pos <= query_pos

    def attend_both(n_full):
        diag = slice(n_full, n_full + tq)
        cols = [slice(sub * HEAD_DIM, (sub + 1) * HEAD_DIM) for sub in (0, 1)]
        qs = [q_ref[:, c] for c in cols]
        s_diag = [jnp.where(causal, _dot_nt(k_ref[diag, c], q), MASK_VALUE)
                  for c, q in zip(cols, qs)]
        s_full = [_dot_nt(k_ref[0:n_full, c], q) for c, q in zip(cols, qs)] if n_full else None
        outs = []
        for sub in (0, 1):
            m = jnp.max(s_diag[sub], axis=0, keepdims=True)
            if n_full:
                m = jnp.maximum(m, jnp.max(s_full[sub], axis=0, keepdims=True))
            p_diag = jnp.exp(s_diag[sub] - m)
            l = jnp.sum(p_diag, axis=0, keepdims=True)
            acc = _dot_tn(v_ref[diag, :], p_diag.astype(BF16))
            if n_full:
                p_full = jnp.exp(s_full[sub] - m)
                l = l + jnp.sum(p_full, axis=0, keepdims=True)
                acc = acc + _dot_tn(v_ref[0:n_full, :], p_full.astype(BF16))
            outs.append(acc / l)
        return outs

    for c in range(nq):
        @pl.when(i == c)
        def _(c=c):
            o0, o1 = attend_both(c * tq)
            o = o0 - lam * o1
            ms = jnp.mean(o * o, axis=0, keepdims=True)
            y = o * lax.rsqrt(ms + RMS_EPS) * subln_ref[...] * (1.0 - lambda_init)
            o_ref[...] = y.T.astype(o_ref.dtype)


def _diff_attention(qkv, lam_vecs, subln, batch, seq, lambda_init, *, tq=512):
    n, width = qkv.shape
    d = width // 3
    heads = d // DA_V_DIM
    tq = _tile(seq, tq)
    nq = seq // tq
    return pl.pallas_call(
        functools.partial(_diff_attn_body, tq=tq, nq=nq, lambda_init=lambda_init),
        out_shape=jax.ShapeDtypeStruct((n, d), BF16),
        grid=(batch, heads, nq),
        in_specs=[
            pl.BlockSpec((4, HEAD_DIM), lambda b, h, i: (0, 0)),
            pl.BlockSpec((DA_V_DIM, 1), lambda b, h, i: (0, 0)),
            pl.BlockSpec((tq, DA_V_DIM), lambda b, h, i: (b * nq + i, h)),
            pl.BlockSpec((seq, DA_V_DIM), lambda b, h, i: (b, heads + h)),
            pl.BlockSpec((seq, DA_V_DIM), lambda b, h, i: (b, 2 * heads + h)),
        ],
        out_specs=pl.BlockSpec((tq, DA_V_DIM), lambda b, h, i: (b * nq + i, h)),
        compiler_params=_params(("parallel", "parallel", "arbitrary")),
        name="diff_attn",
    )(lam_vecs, subln.reshape(DA_V_DIM, 1), qkv, qkv, qkv)


def _beta_decay_body(x_ref, w_ref, alog_ref, dt_ref, o_ref, ot_ref, *, n_heads):
    tm = x_ref.shape[0]
    y = _dot(x_ref[...], w_ref[...])
    a = y + dt_ref[...]
    softplus = jnp.maximum(a, 0.0) + jnp.log1p(jnp.exp(-jnp.abs(a)))
    g = -jnp.exp(alog_ref[...]) * softplus
    row = lax.broadcasted_iota(jnp.int32, (tm, tm), 0)
    col = lax.broadcasted_iota(jnp.int32, (tm, tm), 1)
    same_chunk = (row // CHUNK) == (col // CHUNK)
    prefix = jnp.where(same_chunk & (col <= row), 1.0, 0.0).astype(BF16)
    whole = jnp.where(same_chunk, 1.0, 0.0).astype(BF16)
    hi = g.astype(BF16)
    rest = g - hi.astype(F32)
    mid = rest.astype(BF16)
    lo = (rest - mid.astype(F32)).astype(BF16)
    sums = _dot(jnp.concatenate([prefix, whole], axis=0),
                jnp.concatenate([hi, mid, lo], axis=-1))
    lanes = g.shape[1]
    sums = sums[:, :lanes] + sums[:, lanes:2 * lanes] + sums[:, 2 * lanes:]
    g_cum = sums[:tm]
    g_all = sums[tm:]
    lane = lax.broadcasted_iota(jnp.int32, y.shape, 1)
    packed = jnp.where(lane < n_heads, jax.nn.sigmoid(y),
                       jnp.where(lane < 2 * n_heads, g_cum,
                                 jnp.where(lane < 3 * n_heads, g_all, 0.0)))
    o_ref[...] = packed
    ot_ref[...] = packed.T


def _beta_decay(x, w_b, w_a, a_log, dt_bias, *, tm=256):
    n, d = x.shape
    n_heads = a_log.shape[0]
    lanes = N_BETA_DECAY_LANES
    assert n % tm == 0 and tm % CHUNK == 0 and 3 * n_heads <= lanes
    pad = lanes - 3 * n_heads
    w = jnp.concatenate([w_b, w_a, w_a, jnp.zeros((d, pad), w_a.dtype)], axis=1).astype(BF16)
    zeros = jnp.zeros((n_heads,), F32)
    a_log_l = jnp.concatenate([zeros, a_log, a_log, jnp.zeros((pad,), F32)]).reshape(1, lanes)
    dt_l = jnp.concatenate([zeros, dt_bias, dt_bias, jnp.zeros((pad,), F32)]).reshape(1, lanes)
    return pl.pallas_call(
        functools.partial(_beta_decay_body, n_heads=n_heads),
        out_shape=(jax.ShapeDtypeStruct((n, lanes), F32), jax.ShapeDtypeStruct((lanes, n), F32)),
        grid=(n // tm,),
        in_specs=[
            pl.BlockSpec((tm, d), lambda i: (i, 0)),
            pl.BlockSpec((d, lanes), lambda i: (0, 0)),
            pl.BlockSpec((1, lanes), lambda i: (0, 0)),
            pl.BlockSpec((1, lanes), lambda i: (0, 0)),
        ],
        out_specs=(pl.BlockSpec((tm, lanes), lambda i: (i, 0)),
                   pl.BlockSpec((lanes, tm), lambda i: (0, i))),
        compiler_params=_params(("parallel",)),
        name="beta_decay",
    )(x, w, a_log_l, dt_l)


def _delta_body(q_ref, k_ref, v_ref, z_ref, bd_ref, bdt_ref, nw_ref, o_ref,
                state_ref, qeff_ref, oloc_ref, m_ref, n_ref, g_ref, *,
                tb, n_heads, rep, khs, n_groups):
    s = pl.program_id(1)
    n_pairs = pl.num_programs(1) - 1
    n_v = khs * rep
    n_chunks = tb // CHUNK
    lanes = N_BETA_DECAY_LANES
    prev_group = jnp.maximum(s - 1, 0) % n_groups
    hv0 = (jnp.minimum(s, n_pairs - 1) % n_groups) * n_v

    @pl.when(s == 0)
    def _():
        state_ref[...] = jnp.zeros_like(state_ref)
        qeff_ref[...] = jnp.zeros_like(qeff_ref)
        oloc_ref[...] = jnp.zeros_like(oloc_ref)
        m_ref[...] = jnp.zeros_like(m_ref)
        n_ref[...] = jnp.zeros_like(n_ref)
        g_ref[...] = jnp.zeros_like(g_ref)

    def head_slice(x, i):
        return x[:, i * HEAD_DIM:(i + 1) * HEAD_DIM]

    states = [state_ref[prev_group * n_v + idx] for idx in range(n_v)]
    outs = [[] for _ in range(n_v)]
    for c in range(n_chunks):
        rows = slice(c * CHUNK, (c + 1) * CHUNK)
        for idx in range(n_v):
            lhs = jnp.concatenate([m_ref[idx, c], qeff_ref[idx, rows, :]], axis=0)
            ms = _dot(lhs, states[idx].astype(BF16))
            outs[idx].append(ms[HEAD_DIM:] + oloc_ref[idx, rows, :])
            states[idx] = (states[idx] * g_ref[idx, c][0:1, :] - ms[:HEAD_DIM] + n_ref[idx, c])
    nw = nw_ref[...]
    for idx in range(n_v):
        state_ref[prev_group * n_v + idx] = states[idx]
        o = jnp.concatenate(outs[idx], axis=0)
        z = head_slice(z_ref, idx).astype(F32)
        o = o * lax.rsqrt(jnp.mean(o * o, axis=-1, keepdims=True) + RMS_EPS) * nw * _silu(z)
        o_ref[:, idx * HEAD_DIM:(idx + 1) * HEAD_DIM] = o.astype(o_ref.dtype)

    qs, ks, kks, qks = [], [], [], []
    for a in range(khs):
        q = head_slice(q_ref, a).astype(F32)
        k = head_slice(k_ref, a).astype(F32)
        q = q * lax.rsqrt(jnp.sum(q * q, axis=-1, keepdims=True) + L2_EPS) * (HEAD_DIM ** -0.5)
        k = k * lax.rsqrt(jnp.sum(k * k, axis=-1, keepdims=True) + L2_EPS)
        qs.append(q)
        ks.append(k)
        k_b = k.astype(BF16)
        kks.append(_dot_nt(k_b, k_b))
        qks.append(_dot_nt(q.astype(BF16), k_b))

    row = lax.broadcasted_iota(jnp.int32, (tb, tb), 0)
    col = lax.broadcasted_iota(jnp.int32, (tb, tb), 1)
    same_chunk = (row // CHUNK) == (col // CHUNK)
    tril = same_chunk & (col <= row)
    strict = same_chunk & (col < row)

    bd = pltpu.roll(bd_ref[...], (lanes - hv0) % lanes, 1)

    p_cur, rhs, qk_d, q_g, k_g, g_chunk = [], [], [], [], [], []
    for idx in range(n_v):
        a = idx // rep
        beta_c = bd[:, idx:idx + 1]
        gcum_c = bd[:, n_heads + idx:n_heads + idx + 1]
        gall_c = bd[:, 2 * n_heads + idx:2 * n_heads + idx + 1]
        gcum_r = bdt_ref[pl.ds(n_heads + hv0 + idx, 1), :]
        e = jnp.exp(gcum_c - gcum_r)
        p_cur.append((kks[a] * jnp.where(strict, e, 0.0) * (-beta_c)).astype(BF16))
        qk_d.append((qks[a] * jnp.where(tril, e, 0.0)).astype(BF16))
        exp_g = jnp.exp(gcum_c)
        rhs.append(jnp.concatenate([head_slice(v_ref, idx).astype(F32) * beta_c,
                                    ks[a] * (beta_c * exp_g)], axis=-1))
        q_g.append(qs[a] * exp_g)
        k_g.append((ks[a] * jnp.exp(gall_c - gcum_c)).astype(BF16))
        g_chunk.append(jnp.exp(gall_c))

    n_factors = CHUNK.bit_length() - 1
    blk = HEAD_DIM
    n_blk = tb // blk
    p_cur = [[p[h * blk:(h + 1) * blk, h * blk:(h + 1) * blk] for h in range(n_blk)]
             for p in p_cur]
    rhs = [[r[h * blk:(h + 1) * blk] for h in range(n_blk)] for r in rhs]
    for j in range(n_factors):
        for idx in range(n_v):
            for h in range(n_blk):
                p_b = p_cur[idx][h]
                r_b = rhs[idx][h].astype(BF16)
                if j < n_factors - 1:
                    res = _dot(p_b, jnp.concatenate([p_b, r_b], axis=-1))
                    p_cur[idx][h] = res[:, :blk].astype(BF16)
                    rhs[idx][h] = rhs[idx][h] + res[:, blk:]
                else:
                    rhs[idx][h] = rhs[idx][h] + _dot(p_b, r_b)
    rhs = [jnp.concatenate(r, axis=0) for r in rhs]

    for idx in range(n_v):
        uw_b = rhs[idx].astype(BF16)
        quw = _dot(qk_d[idx], uw_b)
        oloc_ref[idx] = quw[:, :HEAD_DIM]
        qeff_ref[idx] = (q_g[idx] - quw[:, HEAD_DIM:]).astype(BF16)
        for c in range(n_chunks):
            rows = slice(c * CHUNK, (c + 1) * CHUNK)
            kuw = _dot_tn(k_g[idx][rows], uw_b[rows])
            n_ref[idx, c] = kuw[:, :HEAD_DIM]
            m_ref[idx, c] = kuw[:, HEAD_DIM:].astype(BF16)
            g_ref[idx, c] = jnp.broadcast_to(g_chunk[idx][c * CHUNK:c * CHUNK + 1, :],
                                             (SUBLANES, HEAD_DIM))


def _gated_delta(proj, bd, bd_t, norm_w, batch, seq, n_k_heads, n_v_heads, *, tb=256, khs=4):
    n = proj.shape[0]
    rep = n_v_heads // n_k_heads
    khs = _tile(n_k_heads, khs)
    n_groups = n_k_heads // khs
    n_v = khs * rep
    kw = khs * HEAD_DIM
    vw = n_v * HEAD_DIM
    key_dim = n_k_heads * HEAD_DIM
    val_dim = n_v_heads * HEAD_DIM
    assert seq % tb == 0 and tb % HEAD_DIM == 0 and HEAD_DIM % CHUNK == 0
    nt = seq // tb
    n_chunks = tb // CHUNK
    n_pairs = nt * n_groups
    k_blk = key_dim // kw
    v_blk = 2 * key_dim // vw
    z_blk = (2 * key_dim + val_dim) // vw
    lanes = N_BETA_DECAY_LANES

    def prep(s):
        pair = jnp.minimum(s, n_pairs - 1)
        return pair // n_groups, pair % n_groups

    def fin(s):
        pair = jnp.maximum(s - 1, 0)
        return pair // n_groups, pair % n_groups

    return pl.pallas_call(
        functools.partial(_delta_body, tb=tb, n_heads=n_v_heads, rep=rep, khs=khs,
                          n_groups=n_groups),
        out_shape=jax.ShapeDtypeStruct((n, val_dim), BF16),
        grid=(batch, n_pairs + 1),
        in_specs=[
            pl.BlockSpec((tb, kw), lambda b, s: (b * nt + prep(s)[0], prep(s)[1])),
            pl.BlockSpec((tb, kw), lambda b, s: (b * nt + prep(s)[0], k_blk + prep(s)[1])),
            pl.BlockSpec((tb, vw), lambda b, s: (b * nt + prep(s)[0], v_blk + prep(s)[1])),
            pl.BlockSpec((tb, vw), lambda b, s: (b * nt + fin(s)[0], z_blk + fin(s)[1])),
            pl.BlockSpec((tb, lanes), lambda b, s: (b * nt + prep(s)[0], 0)),
            pl.BlockSpec((lanes, tb), lambda b, s: (0, b * nt + prep(s)[0])),
            pl.BlockSpec((1, HEAD_DIM), lambda b, s: (0, 0)),
        ],
        out_specs=pl.BlockSpec((tb, vw), lambda b, s: (b * nt + fin(s)[0], fin(s)[1])),
        scratch_shapes=[
            pltpu.VMEM((n_v_heads, HEAD_DIM, HEAD_DIM), F32),
            pltpu.VMEM((n_v, tb, HEAD_DIM), BF16),
            pltpu.VMEM((n_v, tb, HEAD_DIM), F32),
            pltpu.VMEM((n_v, n_chunks, HEAD_DIM, HEAD_DIM), BF16),
            pltpu.VMEM((n_v, n_chunks, HEAD_DIM, HEAD_DIM), F32),
            pltpu.VMEM((n_v, n_chunks, SUBLANES, HEAD_DIM), F32),
        ],
        compiler_params=_params(("parallel", "arbitrary")),
        name="gated_delta",
    )(proj, proj, proj, proj, bd, bd_t, norm_w.reshape(1, HEAD_DIM))


def _rope_tables(seq):
    inv = 1.0 / (ROPE_THETA ** (jnp.arange(0, HEAD_DIM, 2, dtype=F32) / HEAD_DIM))
    ang = jnp.arange(seq, dtype=F32)[:, None] * inv[None, :]
    ang = jnp.concatenate([ang, ang], axis=-1)
    sign = jnp.where(jnp.arange(HEAD_DIM) < HEAD_DIM // 2, -1.0, 1.0).astype(F32)
    return jnp.cos(ang), jnp.sin(ang) * sign[None, :]


def _diff_attention_mixer(h, xn, w_qkv, layer, lq1, lk1, lq2, lk2, subln, w_o, lambda_init,
                          batch, seq):
    d = h.shape[1]
    cos, sin = _rope_tables(seq)
    qkv = _proj_rope(xn, w_qkv, layer, cos, sin, seq, d, 2 * d, HEAD_DIM ** -0.5)
    lam_vecs = jnp.stack([lq1, lk1, lq2, lk2]).astype(F32)
    attn = _diff_attention(qkv, lam_vecs, subln, batch, seq, lambda_init)
    return _out_proj(attn, w_o, layer, h)


def _gated_deltanet_mixer(h, xn, w_in, conv_w, layer, a_log, dt_bias, gdn_norm, w_o, batch, seq):
    n_v_heads = a_log.shape[0]
    val_dim = n_v_heads * HEAD_DIM
    conv_dim = conv_w.shape[2]
    key_dim = (conv_dim - val_dim) // 2
    n_k_heads = key_dim // HEAD_DIM
    main = conv_dim + val_dim
    proj = _proj_conv(xn, w_in.astype(BF16), conv_w, layer, seq, main)
    bd, bd_t = _beta_decay(xn, w_in[layer, :, main:main + n_v_heads],
                           w_in[layer, :, main + n_v_heads:], a_log, dt_bias)
    o = _gated_delta(proj, bd, bd_t, gdn_norm, batch, seq, n_k_heads, n_v_heads)
    return _out_proj(o, w_o, layer, h)


def kernel(x, ffn1_norm, ffn1_w_gate, ffn1_w_up, ffn1_w_down, mix_norm, ffn2_norm, ffn2_w_gate, ffn2_w_up, ffn2_w_down, da_w_qkv, da_lambda_q1, da_lambda_k1, da_lambda_q2, da_lambda_k2, da_subln, da_w_o, gdn_w_in, gdn_conv_w, gdn_a_log, gdn_dt_bias, gdn_norm, gdn_w_o, final_norm):
    batch, seq, d = x.shape
    depth = ffn1_norm.shape[0]
    h = x.reshape(batch * seq, d)
    for i in range(depth):
        h, xn = _ffn(h, ffn1_norm[i], ffn1_w_gate, ffn1_w_up, ffn1_w_down, i, "next", mix_norm[i])
        j = i // 2
        if i % 2 == 0:
            lambda_init = 0.8 - 0.6 * math.exp(-0.3 * i)
            h = _diff_attention_mixer(h, xn, da_w_qkv, j, da_lambda_q1[j], da_lambda_k1[j],
                                      da_lambda_q2[j], da_lambda_k2[j], da_subln[j], da_w_o,
                                      lambda_init, batch, seq)
        else:
            h = _gated_deltanet_mixer(h, xn, gdn_w_in, gdn_conv_w, j, gdn_a_log[j],
                                      gdn_dt_bias[j], gdn_norm[j], gdn_w_o, batch, seq)
        if i == depth - 1:
            h = _ffn(h, ffn2_norm[i], ffn2_w_gate, ffn2_w_up, ffn2_w_down, i, "final", final_norm)
        else:
            h = _ffn(h, ffn2_norm[i], ffn2_w_gate, ffn2_w_up, ffn2_w_down, i)
    return h.reshape(batch, seq, d)
```
